```python
import math
import jax, jax.numpy as jnp
from jax import lax
import numpy as np

D_MODEL = 2048
BATCH = 32
SEQ = 256
DEPTH = 1
DEC_BATCH = 2
DEC_SEQ = 4096
PAST_LEN = 256

GRID_W = 64
HEAD_DIM = 128
GQA_Q_HEADS = D_MODEL // 256
GQA_KV_HEADS = GQA_Q_HEADS // 4
GQA_GROUP = GQA_Q_HEADS // GQA_KV_HEADS
DIFF_HEADS = D_MODEL // 512
DIFF_V_DIM = 2 * HEAD_DIM
GQA_Q_W = GQA_Q_HEADS * HEAD_DIM
GQA_KV_W = GQA_KV_HEADS * HEAD_DIM
DIFF_QK_W = DIFF_HEADS * 2 * HEAD_DIM
DIFF_V_W = DIFF_HEADS * DIFF_V_DIM
GATE_W = 2 * D_MODEL
PROJ_W = GQA_Q_W + 2 * GQA_KV_W + 2 * DIFF_QK_W + DIFF_V_W + GATE_W
Q_BLOCK = 128
ROPE_THETA = 10000.0
ROPE_PAIRS_PER_AXIS = HEAD_DIM // 4
PEER_HEADS = 8
PEER_N_KEYS = 128
PEER_N_EXPERTS = PEER_N_KEYS * PEER_N_KEYS
PEER_QUERY_DIM = 256
PEER_HALF = PEER_QUERY_DIM // 2
PEER_TOPK = 16
PEER_TOKEN_BLOCK = 128
NORM_EPS = 1e-6

kernel_name = "hybrid_gqa_diffattn_peer_diffusion_step"


def rms_norm(x, g):
    xf = x.astype(jnp.float32)
    y = xf * lax.rsqrt(jnp.mean(xf * xf, axis=-1, keepdims=True) + NORM_EPS)
    return (y * g.astype(jnp.float32)).astype(x.dtype)


def axial_rope_tables(n_tokens):
    rows = n_tokens // GRID_W
    row = jnp.repeat(jnp.arange(rows, dtype=jnp.float32), GRID_W)
    col = jnp.tile(jnp.arange(GRID_W, dtype=jnp.float32), rows)
    inv = ROPE_THETA ** (-jnp.arange(ROPE_PAIRS_PER_AXIS, dtype=jnp.float32) / ROPE_PAIRS_PER_AXIS)
    ar = row[:, None] * inv
    ac = col[:, None] * inv
    ang = jnp.concatenate([ar, ar, ac, ac], axis=-1)
    return jnp.cos(ang), jnp.sin(ang)


def apply_rope(x, cos, sin):
    shape = (1, cos.shape[0]) + (1,) * (x.ndim - 3) + (cos.shape[1],)
    c = cos.reshape(shape).astype(x.dtype)
    s = sin.reshape(shape).astype(x.dtype)
    x1, x2, x3, x4 = jnp.split(x, 4, axis=-1)
    rot = jnp.concatenate([-x2, x1, -x4, x3], axis=-1)
    return x * c + rot * s


def gqa_attention(q, k, v):
    B, Tq = q.shape[:2]
    nb = Tq // Q_BLOCK
    qb = q.reshape(B, nb, Q_BLOCK, GQA_KV_HEADS, GQA_GROUP, HEAD_DIM).transpose(1, 0, 2, 3, 4, 5)
    scale = HEAD_DIM ** -0.5

    def block(qblk):
        s = jnp.einsum('bqhgd,bkhd->bhgqk', qblk, k, preferred_element_type=jnp.float32) * scale
        p = jax.nn.softmax(s, axis=-1).astype(v.dtype)
        return jnp.einsum('bhgqk,bkhd->bqhgd', p, v)

    o = lax.map(block, qb)
    return o.transpose(1, 0, 2, 3, 4, 5).reshape(B, Tq, GQA_Q_W)


def diff_attention(q, k, v, lam):
    B, Tq = q.shape[:2]
    nb = Tq // Q_BLOCK
    qb = q.reshape(B, nb, Q_BLOCK, DIFF_HEADS, 2, HEAD_DIM).transpose(1, 0, 2, 3, 4, 5)
    scale = HEAD_DIM ** -0.5

    def block(qblk):
        s = jnp.einsum('bqhcd,bkhcd->bhcqk', qblk, k, preferred_element_type=jnp.float32) * scale
        p = jax.nn.softmax(s, axis=-1)
        a = (p[:, :, 0] - lam * p[:, :, 1]).astype(v.dtype)
        return jnp.einsum('bhqk,bkhe->bqhe', a, v)

    o = lax.map(block, qb)
    return o.transpose(1, 0, 2, 3, 4).reshape(B, Tq, DIFF_HEADS, DIFF_V_DIM)


def peer_layer(h, w_query, sub_keys, expert_u, expert_v):
    B, T, D = h.shape
    n = B * T
    xt = h.reshape(n, D)
    q = (xt @ w_query).reshape(n, PEER_HEADS, 2, PEER_HALF)
    s = jnp.einsum('nhcd,hckd->nhck', q, sub_keys, preferred_element_type=jnp.float32)
    s1, i1 = lax.top_k(s[:, :, 0], PEER_TOPK)
    s2, i2 = lax.top_k(s[:, :, 1], PEER_TOPK)
    cand_s = (s1[..., :, None] + s2[..., None, :]).reshape(n, PEER_HEADS, PEER_TOPK * PEER_TOPK)
    cand_i = (i1[..., :, None] * PEER_N_KEYS + i2[..., None, :]).reshape(n, PEER_HEADS, PEER_TOPK * PEER_TOPK)
    top_s, pos = lax.top_k(cand_s, PEER_TOPK)
    idx = jnp.take_along_axis(cand_i, pos, axis=-1)
    wts = jax.nn.softmax(top_s, axis=-1)
    nb = n // PEER_TOKEN_BLOCK

    def block(args):
        xb, ib, wb = args
        u = jnp.take(expert_u, ib, axis=0)
        a = jax.nn.gelu(jnp.einsum('nd,nhkd->nhk', xb, u, preferred_element_type=jnp.float32), approximate=False)
        coef = (a * wb).astype(xb.dtype)
        return jnp.einsum('nhk,nhkd->nd', coef, jnp.take(expert_v, ib, axis=0))

    out = lax.map(block, (xt.reshape(nb, PEER_TOKEN_BLOCK, D),
                          idx.reshape(nb, PEER_TOKEN_BLOCK, PEER_HEADS, PEER_TOPK),
                          wts.reshape(nb, PEER_TOKEN_BLOCK, PEER_HEADS, PEER_TOPK)))
    return out.reshape(B, T, D)


def trunk_layer(x, cond, p, layer_idx, rope=None, ctx=None):
    B, T, _ = x.shape
    mod = jax.nn.silu(cond) @ p['w_ada'] + p['b_ada']
    sh1, sc1, g1, sh2, sc2, g2 = [m[:, None, :] for m in jnp.split(mod, 6, axis=-1)]

    h = rms_norm(x, p['norm1_g']) * (1 + sc1) + sh1
    z = h @ p['w_in']
    sizes = (GQA_Q_W, GQA_KV_W, GQA_KV_W, DIFF_QK_W, DIFF_QK_W, DIFF_V_W, GATE_W)
    points = np.cumsum(sizes)[:-1].tolist()
    q_g, k_g, v_g, q_d, k_d, v_d, gate_logits = jnp.split(z, points, axis=-1)
    q_g = rms_norm(q_g.reshape(B, T, GQA_Q_HEADS, HEAD_DIM), p['q_norm_g'])
    k_g = rms_norm(k_g.reshape(B, T, GQA_KV_HEADS, HEAD_DIM), p['k_norm_g'])
    v_g = v_g.reshape(B, T, GQA_KV_HEADS, HEAD_DIM)
    q_d = q_d.reshape(B, T, DIFF_HEADS, 2, HEAD_DIM)
    k_d = k_d.reshape(B, T, DIFF_HEADS, 2, HEAD_DIM)
    v_d = v_d.reshape(B, T, DIFF_HEADS, DIFF_V_DIM)
    own = (k_g, v_g, k_d, v_d)

    if rope is None:
        keys_g, vals_g, keys_d, vals_d = own
    else:
        cos, sin = rope
        q_g = apply_rope(q_g, cos, sin)
        q_d = apply_rope(q_d, cos, sin)
        ck_g, cv_g, ck_d, cv_d = ctx
        keys_g = jnp.concatenate([apply_rope(k_g, cos, sin), ck_g], axis=1)
        vals_g = jnp.concatenate([v_g, cv_g], axis=1)
        keys_d = jnp.concatenate([apply_rope(k_d, cos, sin), ck_d], axis=1)
        vals_d = jnp.concatenate([v_d, cv_d], axis=1)

    o_g = gqa_attention(q_g, keys_g, vals_g)
    lam_init = 0.8 - 0.6 * math.exp(-0.3 * layer_idx)
    lam = (jnp.exp(jnp.sum((p['lq1'] * p['lk1']).astype(jnp.float32)))
           - jnp.exp(jnp.sum((p['lq2'] * p['lk2']).astype(jnp.float32))) + lam_init)
    o_d = diff_attention(q_d, keys_d, vals_d, lam)
    o_d = (rms_norm(o_d, p['subln_g']) * (1 - lam_init)).reshape(B, T, DIFF_V_W)

    gate_a, gate_b = jnp.split(jax.nn.sigmoid(gate_logits), 2, axis=-1)
    merged = gate_a * (o_g @ p['w_branch_a']) + gate_b * (o_d @ p['w_branch_b'])
    x = x + g1 * (merged @ p['w_out'])

    h2 = rms_norm(x, p['norm2_g']) * (1 + sc2) + sh2
    x = x + g2 * peer_layer(h2, p['peer_w_query'], p['peer_sub_keys'], p['peer_u'], p['peer_v'])
    return x, own


def setup_inputs(seed: int = 0) -> dict:
    key = jax.random.key(seed)
    ks = jax.random.split(key, 32)
    f32 = jnp.float32
    D = D_MODEL

    def nrm(k, shape, scale):
        return jax.random.normal(k, shape, f32) * scale

    return {
        'x_prompt': nrm(ks[0], (BATCH, SEQ, D), 1.0),
        'x_sample': nrm(ks[1], (DEC_BATCH, DEC_SEQ, D), 1.0),
        'c': nrm(ks[2], (DEC_BATCH, D), 1.0),
        'cache_gqa_k': nrm(ks[3], (DEC_BATCH, DEPTH, PAST_LEN, GQA_KV_HEADS, HEAD_DIM), 1.0),
        'cache_gqa_v': nrm(ks[4], (DEC_BATCH, DEPTH, PAST_LEN, GQA_KV_HEADS, HEAD_DIM), 1.0),
        'cache_diff_k': nrm(ks[5], (DEC_BATCH, DEPTH, PAST_LEN, DIFF_HEADS, 2, HEAD_DIM), 1.0),
        'cache_diff_v': nrm(ks[6], (DEC_BATCH, DEPTH, PAST_LEN, DIFF_HEADS, DIFF_V_DIM), 1.0),
        'c_ctx': nrm(ks[7], (D,), 1.0),
        'w_ada': nrm(ks[8], (DEPTH, D, 6 * D), 0.5 * D ** -0.5),
        'b_ada': nrm(ks[9], (DEPTH, 6 * D), 0.01),
        'norm1_g': 1.0 + nrm(ks[10], (DEPTH, D), 0.02),
        'norm2_g': 1.0 + nrm(ks[11], (DEPTH, D), 0.02),
        'w_in': nrm(ks[12], (DEPTH, D, PROJ_W), D ** -0.5),
        'q_norm_g': 1.0 + nrm(ks[13], (DEPTH, HEAD_DIM), 0.02),
        'k_norm_g': 1.0 + nrm(ks[14], (DEPTH, HEAD_DIM), 0.02),
        'lambda_q1': nrm(ks[15], (DEPTH, HEAD_DIM), 0.1),
        'lambda_k1': nrm(ks[16], (DEPTH, HEAD_DIM), 0.1),
        'lambda_q2': nrm(ks[17], (DEPTH, HEAD_DIM), 0.1),
        'lambda_k2': nrm(ks[18], (DEPTH, HEAD_DIM), 0.1),
        'diff_subln_g': 1.0 + nrm(ks[19], (DEPTH, DIFF_V_DIM), 0.02),
        'w_branch_a': nrm(ks[20], (DEPTH, GQA_Q_W, D), GQA_Q_W ** -0.5),
        'w_branch_b': nrm(ks[21], (DEPTH, DIFF_V_W, D), DIFF_V_W ** -0.5),
        'w_out': nrm(ks[22], (DEPTH, D, D), D ** -0.5),
        'peer_w_query': nrm(ks[23], (DEPTH, D, PEER_HEADS * PEER_QUERY_DIM), D ** -0.5),
        'peer_sub_keys': nrm(ks[24], (DEPTH, PEER_HEADS, 2, PEER_N_KEYS, PEER_HALF), PEER_HALF ** -0.5),
        'peer_u': nrm(ks[25], (DEPTH, PEER_N_EXPERTS, D), D ** -0.5),
        'peer_v': nrm(ks[26], (DEPTH, PEER_N_EXPERTS, D), PEER_HEADS ** -0.5),
        'final_norm_g': 1.0 + nrm(ks[27], (D,), 0.02),
    }


def reference(x_prompt, x_sample, c, cache_gqa_k, cache_gqa_v, cache_diff_k, cache_diff_v,
              c_ctx, w_ada, b_ada, norm1_g, norm2_g, w_in, q_norm_g, k_norm_g,
              lambda_q1, lambda_k1, lambda_q2, lambda_k2, diff_subln_g,
              w_branch_a, w_branch_b, w_out, peer_w_query, peer_sub_keys, peer_u, peer_v,
              final_norm_g):
    rope = axial_rope_tables(x_sample.shape[1])
    xp = x_prompt
    xs = x_sample
    new_k_g, new_v_g, new_k_d, new_v_d = [], [], [], []
    for l in range(DEPTH):
        p = {
            'w_ada': w_ada[l], 'b_ada': b_ada[l], 'norm1_g': norm1_g[l], 'norm2_g': norm2_g[l],
            'w_in': w_in[l], 'q_norm_g': q_norm_g[l], 'k_norm_g': k_norm_g[l],
            'lq1': lambda_q1[l], 'lk1': lambda_k1[l], 'lq2': lambda_q2[l], 'lk2': lambda_k2[l],
            'subln_g': diff_subln_g[l], 'w_branch_a': w_branch_a[l], 'w_branch_b': w_branch_b[l],
            'w_out': w_out[l], 'peer_w_query': peer_w_query[l], 'peer_sub_keys': peer_sub_keys[l],
            'peer_u': peer_u[l], 'peer_v': peer_v[l],
        }
        xp, (k_g, v_g, k_d, v_d) = trunk_layer(xp, c_ctx[None, :], p, l)
        new_k_g.append(k_g)
        new_v_g.append(v_g)
        new_k_d.append(k_d)
        new_v_d.append(v_d)
        ctx = (cache_gqa_k[:, l], cache_gqa_v[:, l], cache_diff_k[:, l], cache_diff_v[:, l])
        xs, _ = trunk_layer(xs, c, p, l, rope=rope, ctx=ctx)
    y_prompt = rms_norm(xp, final_norm_g)
    y_sample = rms_norm(xs, final_norm_g)
    new_gqa_k = jnp.stack(new_k_g, axis=1)
    new_gqa_v = jnp.stack(new_v_g, axis=1)
    new_diff_k = jnp.stack(new_k_d, axis=1)
    new_diff_v = jnp.stack(new_v_d, axis=1)
    return (y_prompt, y_sample, new_gqa_k, new_gqa_v, new_diff_k, new_diff_v)
```

```python
import functools
import math

import jax
import jax.numpy as jnp
from jax import lax
from jax.experimental import pallas as pl
from jax.experimental.pallas import tpu as pltpu

F32 = jnp.float32
BF16 = jnp.bfloat16

D_MODEL = 2048
HEAD_DIM = 128
GRID_W = 64
GQA_Q_HEADS = 8
GQA_KV_HEADS = 2
GQA_GROUP = GQA_Q_HEADS // GQA_KV_HEADS
DIFF_HEADS = 4
DIFF_V_DIM = 2 * HEAD_DIM
GQA_Q_W = GQA_Q_HEADS * HEAD_DIM
GQA_KV_W = GQA_KV_HEADS * HEAD_DIM
DIFF_QK_W = DIFF_HEADS * 2 * HEAD_DIM
DIFF_V_W = DIFF_HEADS * DIFF_V_DIM
GATE_W = 2 * D_MODEL
PROJ_W = GQA_Q_W + 2 * GQA_KV_W + 2 * DIFF_QK_W + DIFF_V_W + GATE_W
ROPE_THETA = 10000.0
ROPE_PAIRS_PER_AXIS = HEAD_DIM // 4
PEER_HEADS = 8
PEER_N_KEYS = 128
PEER_N_EXPERTS = PEER_N_KEYS * PEER_N_KEYS
PEER_HALF = 128
PEER_TOPK = 16
NORM_EPS = 1e-6
ATTN_SCALE = HEAD_DIM ** -0.5
LAMBDA_INIT = 0.8 - 0.6 * math.exp(-0.3 * 0)

LANES = 128
SUBLANES = 8
BF16_ROWS = 16
MIB = 1024 * 1024
VMEM_LIMIT = 56 * MIB

INPROJ_TM = 1024
INPROJ_TN = 256
ATTN_TQ = 256
MERGE_TM = 256
SCORE_TB = 256
DENSE_NB = 512
DENSE_EB = 512
DENSE_RB = BF16_ROWS
FINAL_TM = 512

_SEG_WIDTHS = (GQA_Q_W, GQA_KV_W, GQA_KV_W, DIFF_QK_W, DIFF_QK_W, DIFF_V_W, GATE_W)


def _params(n_axes):
    return pltpu.CompilerParams(dimension_semantics=("arbitrary",) * n_axes, vmem_limit_bytes=VMEM_LIMIT)


def _rms(x):
    return x * lax.rsqrt(jnp.mean(x * x, axis=-1, keepdims=True) + NORM_EPS)


def _split_bf16(x):
    hi = x.astype(BF16)
    lo = (x - hi.astype(F32)).astype(BF16)
    return hi, lo


def _ada_kernel(cond_ref, w_ref, b_ref, o_ref):
    c = cond_ref[...]
    s = c * jax.nn.sigmoid(c)
    s_hi, s_lo = _split_bf16(s)
    w_hi, w_lo = _split_bf16(w_ref[...])
    acc = jnp.dot(s_hi, w_hi, preferred_element_type=F32)
    acc += jnp.dot(s_hi, w_lo, preferred_element_type=F32)
    acc += jnp.dot(s_lo, w_hi, preferred_element_type=F32)
    o_ref[...] = acc + b_ref[...]


def _ada(cond, w_ada, b_ada):
    rows, d = cond.shape
    n = w_ada.shape[1]
    tn = 1024
    return pl.pallas_call(
        _ada_kernel,
        grid=(n // tn,),
        in_specs=[pl.BlockSpec((rows, d), lambda j: (0, 0)),
                  pl.BlockSpec((d, tn), lambda j: (0, j)),
                  pl.BlockSpec((1, tn), lambda j: (0, j))],
        out_specs=pl.BlockSpec((rows, tn), lambda j: (0, j)),
        out_shape=jax.ShapeDtypeStruct((rows, n), F32),
        compiler_params=_params(1),
        name="ada",
    )(cond, w_ada, b_ada)


def _rope(x, cos, sin_signed, even_chunk):
    up = pltpu.roll(x, HEAD_DIM - HEAD_DIM // 4, axis=1)
    down = pltpu.roll(x, HEAD_DIM // 4, axis=1)
    return x * cos + jnp.where(even_chunk, up, down) * sin_signed


def _inproj_kernel(*refs, rope, seg_starts):
    if rope:
        (x_ref, mod_ref, g1_ref, w_ref, qn_ref, kn_ref, cos_ref, sin_ref,
         qg_ref, kg_ref, vg_ref, qd_ref, kd_ref, vd_ref, gate_ref, h_scr) = refs
    else:
        (x_ref, mod_ref, g1_ref, w_ref, qn_ref, kn_ref,
         qg_ref, kg_ref, vg_ref, qd_ref, kd_ref, vd_ref, gate_ref, h_scr) = refs
    j = pl.program_id(1)

    @pl.when(j == 0)
    def _():
        y = _rms(x_ref[...]) * g1_ref[...]
        h_scr[...] = (y * (1.0 + mod_ref[0, 1:2, :]) + mod_ref[0, 0:1, :]).astype(BF16)

    z = jnp.dot(h_scr[...], w_ref[...], preferred_element_type=F32)
    heads = [z[:, k * HEAD_DIM:(k + 1) * HEAD_DIM] for k in range(INPROJ_TN // HEAD_DIM)]

    if rope:
        cos = cos_ref[...]
        sin_signed = sin_ref[...]
        lane = lax.broadcasted_iota(jnp.int32, cos.shape, 1)
        even_chunk = (lane // (HEAD_DIM // 4)) % 2 == 0
        maybe_rope = lambda t: _rope(t, cos, sin_signed, even_chunk)
    else:
        maybe_rope = lambda t: t

    def store(ref, parts):
        for k, p in enumerate(parts):
            ref[:, k * HEAD_DIM:(k + 1) * HEAD_DIM] = p.astype(ref.dtype)

    def in_seg(s):
        return jnp.logical_and(j >= seg_starts[s], j < seg_starts[s + 1])

    @pl.when(in_seg(0))
    def _():
        store(qg_ref, [maybe_rope(_rms(t) * qn_ref[...]) * ATTN_SCALE for t in heads])

    @pl.when(in_seg(1))
    def _():
        store(kg_ref, [maybe_rope(_rms(t) * kn_ref[...]) for t in heads])

    @pl.when(in_seg(2))
    def _():
        store(vg_ref, heads)

    @pl.when(in_seg(3))
    def _():
        store(qd_ref, [maybe_rope(t) * ATTN_SCALE for t in heads])

    @pl.when(in_seg(4))
    def _():
        store(kd_ref, [maybe_rope(t) for t in heads])

    @pl.when(in_seg(5))
    def _():
        store(vd_ref, heads)

    @pl.when(in_seg(6))
    def _():
        store(gate_ref, [jax.nn.sigmoid(t) for t in heads])


def _inproj(x, mod, mod_row_fn, g1, w_in, qn, kn, rope_tables, kv_dtype):
    n, d = x.shape
    tm, tn = INPROJ_TM, INPROJ_TN
    seg_blocks = [w // tn for w in _SEG_WIDTHS]
    seg_starts = [0]
    for b in seg_blocks:
        seg_starts.append(seg_starts[-1] + b)
    rope = rope_tables is not None

    def seg_spec(s):
        start, nblk = seg_starts[s], seg_blocks[s]
        return pl.BlockSpec((tm, tn), lambda i, j: (i, jnp.clip(j - start, 0, nblk - 1)))

    in_specs = [pl.BlockSpec((tm, d), lambda i, j: (i, 0)),
                pl.BlockSpec((1, 6, d), lambda i, j: (mod_row_fn(i), 0, 0)),
                pl.BlockSpec((1, d), lambda i, j: (0, 0)),
                pl.BlockSpec((d, tn), lambda i, j: (0, j)),
                pl.BlockSpec((1, HEAD_DIM), lambda i, j: (0, 0)),
                pl.BlockSpec((1, HEAD_DIM), lambda i, j: (0, 0))]
    args = [x, mod, g1, w_in, qn, kn]
    if rope:
        cos, sin_signed = rope_tables
        blocks_per_seq = cos.shape[0] // tm
        in_specs += [pl.BlockSpec((tm, HEAD_DIM), lambda i, j: (i % blocks_per_seq, 0))] * 2
        args += [cos, sin_signed]
    out_dtypes = (BF16, kv_dtype, kv_dtype, BF16, kv_dtype, kv_dtype, BF16)
    out_shape = [jax.ShapeDtypeStruct((n, w), dt) for w, dt in zip(_SEG_WIDTHS, out_dtypes)]
    return pl.pallas_call(
        functools.partial(_inproj_kernel, rope=rope, seg_starts=tuple(seg_starts)),
        grid=(n // tm, PROJ_W // tn),
        in_specs=in_specs,
        out_specs=[seg_spec(s) for s in range(len(_SEG_WIDTHS))],
        out_shape=out_shape,
        scratch_shapes=[pltpu.VMEM((tm, d), BF16)],
        compiler_params=_params(2),
        name="inproj_rope" if rope else "inproj",
    )(*args)


def _nt_dot(a, b):
    return lax.dot_general(a, b, (((1,), (1,)), ((), ())), preferred_element_type=F32)


def _softmax_parts(scores):
    m = scores[0].max(axis=-1, keepdims=True)
    for s in scores[1:]:
        m = jnp.maximum(m, s.max(axis=-1, keepdims=True))
    exps = [jnp.exp(s - m) for s in scores]
    total = exps[0].sum(axis=-1, keepdims=True)
    for e in exps[1:]:
        total = total + e.sum(axis=-1, keepdims=True)
    return exps, 1.0 / total


def _subln(o, g):
    return (_rms(o) * g) * (1.0 - LAMBDA_INIT)


def _ctx_attn_kernel(lam_ref, qg_ref, kg_ref, vg_ref, qd_ref, kd_ref, vd_ref, subg_ref, og_ref, od_ref):
    lam = lam_ref[0]
    for kvh in range(GQA_KV_HEADS):
        cols = slice(kvh * HEAD_DIM, (kvh + 1) * HEAD_DIM)
        k = kg_ref[:, cols].astype(BF16)
        v = vg_ref[:, cols].astype(BF16)
        for g in range(GQA_GROUP):
            hc = slice((kvh * GQA_GROUP + g) * HEAD_DIM, (kvh * GQA_GROUP + g + 1) * HEAD_DIM)
            (e,), inv = _softmax_parts([_nt_dot(qg_ref[:, hc], k)])
            o = jnp.dot(e.astype(BF16), v, preferred_element_type=F32) * inv
            og_ref[:, hc] = o.astype(og_ref.dtype)
    for h in range(DIFF_HEADS):
        c1 = slice((2 * h) * HEAD_DIM, (2 * h + 1) * HEAD_DIM)
        c2 = slice((2 * h + 1) * HEAD_DIM, (2 * h + 2) * HEAD_DIM)
        vc = slice(h * DIFF_V_DIM, (h + 1) * DIFF_V_DIM)
        (e1,), inv1 = _softmax_parts([_nt_dot(qd_ref[:, c1], kd_ref[:, c1].astype(BF16))])
        (e2,), inv2 = _softmax_parts([_nt_dot(qd_ref[:, c2], kd_ref[:, c2].astype(BF16))])
        a = e1 * inv1 - (lam * inv2) * e2
        o = jnp.dot(a.astype(BF16), vd_ref[:, vc].astype(BF16), preferred_element_type=F32)
        od_ref[:, vc] = _subln(o, subg_ref[...]).astype(od_ref.dtype)


def _ctx_attention(lam, qg, kg, vg, qd, kd, vd, subg, seq):
    n = qg.shape[0]
    row = lambda w: pl.BlockSpec((seq, w), lambda b: (b, 0))
    return pl.pallas_call(
        _ctx_attn_kernel,
        grid=(n // seq,),
        in_specs=[pl.BlockSpec(memory_space=pltpu.SMEM),
                  row(GQA_Q_W), row(GQA_KV_W), row(GQA_KV_W), row(DIFF_QK_W), row(DIFF_QK_W), row(DIFF_V_W),
                  pl.BlockSpec((1, DIFF_V_DIM), lambda b: (0, 0))],
        out_specs=[row(GQA_Q_W), row(DIFF_V_W)],
        out_shape=[jax.ShapeDtypeStruct((n, GQA_Q_W), BF16), jax.ShapeDtypeStruct((n, DIFF_V_W), BF16)],
        compiler_params=_params(1),
        name="ctx_attn",
    )(lam, qg, kg, vg, qd, kd, vd, subg)


def _lat_gqa_kernel(q_ref, k_ref, v_ref, ck_ref, cv_ref, o_ref):
    k = k_ref[...]
    v = v_ref[...]
    ck = ck_ref[...].astype(BF16)
    cv = cv_ref[...].astype(BF16)
    for g in range(GQA_GROUP):
        hc = slice(g * HEAD_DIM, (g + 1) * HEAD_DIM)
        q = q_ref[:, hc]
        (e_l, e_c), inv = _softmax_parts([_nt_dot(q, k), _nt_dot(q, ck)])
        o = jnp.dot(e_l.astype(BF16), v, preferred_element_type=F32)
        o += jnp.dot(e_c.astype(BF16), cv, preferred_element_type=F32)
        o_ref[:, hc] = (o * inv).astype(o_ref.dtype)


def _lat_gqa_attention(qg, kg, vg, cache_k, cache_v, n_batch, seq):
    tq = ATTN_TQ
    nq = seq // tq
    past = cache_k.shape[1]
    gw = GQA_GROUP * HEAD_DIM
    return pl.pallas_call(
        _lat_gqa_kernel,
        grid=(n_batch, GQA_KV_HEADS, nq),
        in_specs=[pl.BlockSpec((tq, gw), lambda b, h, q: (b * nq + q, h)),
                  pl.BlockSpec((seq, HEAD_DIM), lambda b, h, q: (b, h)),
                  pl.BlockSpec((seq, HEAD_DIM), lambda b, h, q: (b, h)),
                  pl.BlockSpec((None, past, HEAD_DIM), lambda b, h, q: (b, 0, h)),
                  pl.BlockSpec((None, past, HEAD_DIM), lambda b, h, q: (b, 0, h))],
        out_specs=pl.BlockSpec((tq, gw), lambda b, h, q: (b * nq + q, h)),
        out_shape=jax.ShapeDtypeStruct(qg.shape, BF16),
        compiler_params=_params(3),
        name="lat_gqa",
    )(qg, kg, vg, cache_k, cache_v)


def _lat_diff_kernel(lam_ref, q_ref, k_ref, v_ref, ck_ref, cv_ref, subg_ref, o_ref):
    lam = lam_ref[0]
    c1 = slice(0, HEAD_DIM)
    c2 = slice(HEAD_DIM, 2 * HEAD_DIM)
    ck = ck_ref[...].astype(BF16)
    (e1_l, e1_c), inv1 = _softmax_parts([_nt_dot(q_ref[:, c1], k_ref[:, c1]), _nt_dot(q_ref[:, c1], ck[:, c1])])
    (e2_l, e2_c), inv2 = _softmax_parts([_nt_dot(q_ref[:, c2], k_ref[:, c2]), _nt_dot(q_ref[:, c2], ck[:, c2])])
    w2 = lam * inv2
    a_l = e1_l * inv1 - w2 * e2_l
    a_c = e1_c * inv1 - w2 * e2_c
    o = jnp.dot(a_l.astype(BF16), v_ref[...], preferred_element_type=F32)
    o += jnp.dot(a_c.astype(BF16), cv_ref[...].astype(BF16), preferred_element_type=F32)
    o_ref[...] = _subln(o, subg_ref[...]).astype(o_ref.dtype)


def _lat_diff_attention(lam, qd, kd, vd, cache_k, cache_v, subg, n_batch, seq):
    tq = ATTN_TQ
    nq = seq // tq
    past = cache_k.shape[1]
    hw = 2 * HEAD_DIM
    return pl.pallas_call(
        _lat_diff_kernel,
        grid=(n_batch, DIFF_HEADS, nq),
        in_specs=[pl.BlockSpec(memory_space=pltpu.SMEM),
                  pl.BlockSpec((tq, hw), lambda b, h, q: (b * nq + q, h)),
                  pl.BlockSpec((seq, hw), lambda b, h, q: (b, h)),
                  pl.BlockSpec((seq, DIFF_V_DIM), lambda b, h, q: (b, h)),
                  pl.BlockSpec((None, past, hw), lambda b, h, q: (b, 0, h)),
                  pl.BlockSpec((None, past, DIFF_V_DIM), lambda b, h, q: (b, 0, h)),
                  pl.BlockSpec((1, DIFF_V_DIM), lambda b, h, q: (0, 0))],
        out_specs=pl.BlockSpec((tq, DIFF_V_DIM), lambda b, h, q: (b * nq + q, h)),
        out_shape=jax.ShapeDtypeStruct(qd.shape, BF16),
        compiler_params=_params(3),
        name="lat_diff",
    )(lam, qd, kd, vd, cache_k, cache_v, subg)


def _merge_kernel(x_ref, og_ref, od_ref, gate_ref, wa_ref, wb_ref, wo_ref, mod_ref, g2_ref, x1_ref, h2t_ref):
    ya = jnp.dot(og_ref[...], wa_ref[...], preferred_element_type=F32)
    yb = jnp.dot(od_ref[...], wb_ref[...], preferred_element_type=F32)
    merged = gate_ref[:, :D_MODEL].astype(F32) * ya + gate_ref[:, D_MODEL:].astype(F32) * yb
    y = jnp.dot(merged.astype(BF16), wo_ref[...], preferred_element_type=F32)
    x1 = x_ref[...] + mod_ref[0, 2:3, :] * y
    x1_ref[...] = x1
    h2 = (_rms(x1) * g2_ref[...]) * (1.0 + mod_ref[0, 4:5, :]) + mod_ref[0, 3:4, :]
    h2t_ref[...] = h2.T.astype(BF16)


def _merge(x, og, od, gates, wa, wb, wo, mod, mod_row_fn, g2):
    n, d = x.shape
    tm = MERGE_TM
    resident = lambda shape: pl.BlockSpec(shape, lambda i: (0, 0), pipeline_mode=pl.Buffered(1))
    return pl.pallas_call(
        _merge_kernel,
        grid=(n // tm,),
        in_specs=[pl.BlockSpec((tm, d), lambda i: (i, 0)),
                  pl.BlockSpec((tm, GQA_Q_W), lambda i: (i, 0)),
                  pl.BlockSpec((tm, DIFF_V_W), lambda i: (i, 0)),
                  pl.BlockSpec((tm, GATE_W), lambda i: (i, 0)),
                  resident(wa.shape), resident(wb.shape), resident(wo.shape),
                  pl.BlockSpec((1, 6, d), lambda i: (mod_row_fn(i), 0, 0)),
                  pl.BlockSpec((1, d), lambda i: (0, 0))],
        out_specs=[pl.BlockSpec((tm, d), lambda i: (i, 0)),
                   pl.BlockSpec((d, tm), lambda i: (0, i))],
        out_shape=[jax.ShapeDtypeStruct((n, d), F32), jax.ShapeDtypeStruct((d, n), BF16)],
        compiler_params=_params(1),
        name="merge",
    )(x, og, od, gates, wa, wb, wo, mod, g2)


def _top_values(s, k):
    rows = s.shape[0]
    iota = lax.broadcasted_iota(jnp.int32, s.shape, 0).astype(F32)
    vals = []
    for r in range(k):
        m = jnp.max(s, axis=0, keepdims=True)
        vals.append(m)
        if r + 1 < k:
            first = jnp.min(jnp.where(s == m, iota, float(rows)), axis=0, keepdims=True)
            s = jnp.where(iota == first, -jnp.inf, s)
    return jnp.concatenate(vals, axis=0)


def _peer_score_kernel(h2t_ref, wqt_ref, keys_ref, s_ref, e_ref, tau_ref):
    qt = jnp.dot(wqt_ref[...], h2t_ref[...], preferred_element_type=F32).astype(BF16)
    taus = []
    for h in range(PEER_HEADS):
        r1 = slice((2 * h) * PEER_N_KEYS, (2 * h + 1) * PEER_N_KEYS)
        r2 = slice((2 * h + 1) * PEER_N_KEYS, (2 * h + 2) * PEER_N_KEYS)
        s1 = jnp.dot(keys_ref[2 * h], qt[r1], preferred_element_type=F32)
        s2 = jnp.dot(keys_ref[2 * h + 1], qt[r2], preferred_element_type=F32)
        top1 = _top_values(s1, PEER_TOPK)
        top2 = _top_values(s2, PEER_TOPK)
        cand = jnp.concatenate([top1[r:r + 1] + top2 for r in range(PEER_TOPK)], axis=0)
        tops = _top_values(cand, PEER_TOPK)
        z = jnp.sum(jnp.exp(tops - tops[0:1]), axis=0, keepdims=True)
        s_ref[r1, :] = s1
        s_ref[r2, :] = s2
        e_ref[r1, :] = jnp.exp(s1 - top1[0:1])
        e_ref[r2, :] = jnp.exp(s2 - top2[0:1]) * (1.0 / z)
        taus.append(tops[PEER_TOPK - 1:PEER_TOPK])
    tau_ref[...] = jnp.concatenate(taus, axis=0)


def _peer_scores(h2t, wqt, keys):
    d, n = h2t.shape
    tb = SCORE_TB
    rows = PEER_HEADS * 2 * PEER_N_KEYS
    return pl.pallas_call(
        _peer_score_kernel,
        grid=(n // tb,),
        in_specs=[pl.BlockSpec((d, tb), lambda i: (0, i)),
                  pl.BlockSpec(wqt.shape, lambda i: (0, 0), pipeline_mode=pl.Buffered(1)),
                  pl.BlockSpec(keys.shape, lambda i: (0, 0, 0), pipeline_mode=pl.Buffered(1))],
        out_specs=[pl.BlockSpec((rows, tb), lambda i: (0, i)),
                   pl.BlockSpec((rows, tb), lambda i: (0, i)),
                   pl.BlockSpec((PEER_HEADS, tb), lambda i: (0, i))],
        out_shape=[jax.ShapeDtypeStruct((rows, n), F32),
                   jax.ShapeDtypeStruct((rows, n), F32),
                   jax.ShapeDtypeStruct((PEER_HEADS, n), F32)],
        compiler_params=_params(1),
        name="peer_score",
    )(h2t, wqt, keys)


def _peer_dense_kernel(h2t_ref, u_ref, vt_ref, s_ref, e_ref, tau_ref, o_ref, acc_ref, at_ref, coef_ref):
    step = pl.program_id(1)
    eb, nb = at_ref.shape
    at_ref[...] = jnp.dot(u_ref[...], h2t_ref[...], preferred_element_type=F32)

    a0 = step * (eb // PEER_N_KEYS)
    for al in range(eb // PEER_N_KEYS):
        a = a0 + al
        for t in range(nb // LANES):
            ls = slice(t * LANES, (t + 1) * LANES)
            a_group = pl.multiple_of((a // SUBLANES) * SUBLANES, SUBLANES)
            pick = lax.broadcasted_iota(jnp.int32, (SUBLANES, LANES), 0) == a % SUBLANES

            def key_row(ref, h):
                group = ref[pl.ds((2 * h) * PEER_N_KEYS + a_group, SUBLANES), ls]
                return jnp.sum(jnp.where(pick, group, 0.0), axis=0, keepdims=True)

            s1b = [key_row(s_ref, h) for h in range(PEER_HEADS)]
            e1b = [key_row(e_ref, h) for h in range(PEER_HEADS)]
            taub = [tau_ref[h:h + 1, ls] for h in range(PEER_HEADS)]

            def chunk(c, carry):
                r0 = pl.multiple_of(c * DENSE_RB, DENSE_RB)
                w = jnp.zeros((DENSE_RB, LANES), F32)
                for h in range(PEER_HEADS):
                    rows = pl.ds((2 * h + 1) * PEER_N_KEYS + r0, DENSE_RB)
                    total = s1b[h] + s_ref[rows, ls]
                    w = w + jnp.where(total >= taub[h], e1b[h] * e_ref[rows, ls], 0.0)
                out_rows = pl.ds(al * PEER_N_KEYS + r0, DENSE_RB)
                pre = at_ref[out_rows, ls]
                act = 0.5 * pre * (1.0 + lax.erf(pre * math.sqrt(0.5)))
                coef_ref[out_rows, ls] = (act * w).astype(BF16)
                return carry

            lax.fori_loop(0, PEER_N_KEYS // DENSE_RB, chunk, 0)

    prod = jnp.dot(vt_ref[...], coef_ref[...], preferred_element_type=F32)

    @pl.when(step == 0)
    def _():
        acc_ref[...] = prod

    @pl.when(step > 0)
    def _():
        acc_ref[...] += prod

    @pl.when(step == pl.num_programs(1) - 1)
    def _():
        o_ref[...] = acc_ref[...].T


def _peer_dense(h2t, u, vt, s, e, tau):
    d, n = h2t.shape
    nb, eb = DENSE_NB, DENSE_EB
    rows = s.shape[0]
    return pl.pallas_call(
        _peer_dense_kernel,
        grid=(n // nb, PEER_N_EXPERTS // eb),
        in_specs=[pl.BlockSpec((d, nb), lambda i, j: (0, i)),
                  pl.BlockSpec((eb, d), lambda i, j: (j, 0)),
                  pl.BlockSpec((d, eb), lambda i, j: (0, j)),
                  pl.BlockSpec((rows, nb), lambda i, j: (0, i)),
                  pl.BlockSpec((rows, nb), lambda i, j: (0, i)),
                  pl.BlockSpec((PEER_HEADS, nb), lambda i, j: (0, i))],
        out_specs=pl.BlockSpec((nb, d), lambda i, j: (i, 0)),
        out_shape=jax.ShapeDtypeStruct((n, d), F32),
        scratch_shapes=[pltpu.VMEM((d, nb), F32), pltpu.VMEM((eb, nb), F32), pltpu.VMEM((eb, nb), BF16)],
        compiler_params=_params(2),
        name="peer_dense",
    )(h2t, u, vt, s, e, tau)


def _final_kernel(x1_ref, p_ref, mod_ref, g_ref, o_ref):
    x2 = x1_ref[...] + mod_ref[0, 5:6, :] * p_ref[...]
    o_ref[...] = _rms(x2) * g_ref[...]


def _final(x1, peer, mod, mod_row_fn, g):
    n, d = x1.shape
    tm = FINAL_TM
    return pl.pallas_call(
        _final_kernel,
        grid=(n // tm,),
        in_specs=[pl.BlockSpec((tm, d), lambda i: (i, 0)),
                  pl.BlockSpec((tm, d), lambda i: (i, 0)),
                  pl.BlockSpec((1, 6, d), lambda i: (mod_row_fn(i), 0, 0)),
                  pl.BlockSpec((1, d), lambda i: (0, 0))],
        out_specs=pl.BlockSpec((tm, d), lambda i: (i, 0)),
        out_shape=jax.ShapeDtypeStruct((n, d), F32),
        compiler_params=_params(1),
        name="final",
    )(x1, peer, mod, g)


def _rope_tables(n_tokens):
    rows = n_tokens // GRID_W
    row = jnp.repeat(jnp.arange(rows, dtype=F32), GRID_W)
    col = jnp.tile(jnp.arange(GRID_W, dtype=F32), rows)
    inv = ROPE_THETA ** (-jnp.arange(ROPE_PAIRS_PER_AXIS, dtype=F32) / ROPE_PAIRS_PER_AXIS)
    ar = row[:, None] * inv
    ac = col[:, None] * inv
    ang = jnp.concatenate([ar, ar, ac, ac], axis=-1)
    sign = jnp.where((jnp.arange(HEAD_DIM) // (HEAD_DIM // 4)) % 2 == 0, -1.0, 1.0).astype(F32)
    return jnp.cos(ang), jnp.sin(ang) * sign


def _trunk(x, mod, rows_per_mod, first_mod_row, w, rope_tables, attention):
    def mod_row_for(tm):
        return lambda i: first_mod_row + (i * tm) // rows_per_mod

    kv_dtype = BF16 if rope_tables is not None else F32
    qg, kg, vg, qd, kd, vd, gates = _inproj(x, mod, mod_row_for(INPROJ_TM), w["norm1_g"], w["w_in"],
                                            w["q_norm_g"], w["k_norm_g"], rope_tables, kv_dtype)
    og, od = attention(qg, kg, vg, qd, kd, vd)
    x1, h2t = _merge(x, og, od, gates, w["w_a"], w["w_b"], w["w_o"], mod, mod_row_for(MERGE_TM), w["norm2_g"])
    s, e, tau = _peer_scores(h2t, w["wq_t"], w["keys"])
    peer = _peer_dense(h2t, w["u"], w["v_t"], s, e, tau)
    y = _final(x1, peer, mod, mod_row_for(FINAL_TM), w["final_g"])
    return y, (kg, vg, kd, vd)


def kernel(x_prompt, x_sample, c, cache_gqa_k, cache_gqa_v, cache_diff_k, cache_diff_v, c_ctx, w_ada, b_ada, norm1_g, norm2_g, w_in, q_norm_g, k_norm_g, lambda_q1, lambda_k1, lambda_q2, lambda_k2, diff_subln_g, w_branch_a, w_branch_b, w_out, peer_w_query, peer_sub_keys, peer_u, peer_v, final_norm_g):
    assert w_in.shape[0] == 1, "one trunk layer"
    batch, seq, d = x_prompt.shape
    dec_batch, dec_seq, _ = x_sample.shape
    past = cache_gqa_k.shape[2]
    l = 0

    w = {
        "norm1_g": norm1_g[l][None, :], "norm2_g": norm2_g[l][None, :], "final_g": final_norm_g[None, :],
        "q_norm_g": q_norm_g[l][None, :], "k_norm_g": k_norm_g[l][None, :],
        "w_in": w_in[l].astype(BF16),
        "w_a": w_branch_a[l].astype(BF16), "w_b": w_branch_b[l].astype(BF16), "w_o": w_out[l].astype(BF16),
        "wq_t": peer_w_query[l].T.astype(BF16),
        "keys": peer_sub_keys[l].reshape(PEER_HEADS * 2, PEER_N_KEYS, PEER_HALF).astype(BF16),
        "u": peer_u[l].astype(BF16),
        "v_t": peer_v[l].T.astype(BF16),
    }
    subg = diff_subln_g[l][None, :]
    lam = (jnp.exp(jnp.sum((lambda_q1[l] * lambda_k1[l]).astype(F32)))
           - jnp.exp(jnp.sum((lambda_q2[l] * lambda_k2[l]).astype(F32))) + LAMBDA_INIT).reshape(1)

    cond = jnp.concatenate([c_ctx[None, :], c], axis=0)
    n_cond = cond.shape[0]
    cond = jnp.pad(cond, ((0, SUBLANES - n_cond), (0, 0)))
    mod = _ada(cond, w_ada[l], b_ada[l][None, :])[:n_cond].reshape(n_cond, 6, d)

    def ctx_attention(qg, kg, vg, qd, kd, vd):
        return _ctx_attention(lam, qg, kg, vg, qd, kd, vd, subg, seq)

    ck_g = cache_gqa_k[:, l].reshape(dec_batch, past, GQA_KV_W)
    cv_g = cache_gqa_v[:, l].reshape(dec_batch, past, GQA_KV_W)
    ck_d = cache_diff_k[:, l].reshape(dec_batch, past, DIFF_QK_W)
    cv_d = cache_diff_v[:, l].reshape(dec_batch, past, DIFF_V_W)

    def lat_attention(qg, kg, vg, qd, kd, vd):
        og = _lat_gqa_attention(qg, kg, vg, ck_g, cv_g, dec_batch, dec_seq)
        od = _lat_diff_attention(lam, qd, kd, vd, ck_d, cv_d, subg, dec_batch, dec_seq)
        return og, od

    n_ctx = batch * seq
    yp, (kg, vg, kd, vd) = _trunk(x_prompt.reshape(n_ctx, d), mod, n_ctx, 0, w, None, ctx_attention)
    ys, _ = _trunk(x_sample.reshape(dec_batch * dec_seq, d), mod, dec_seq, 1, w,
                   _rope_tables(dec_seq), lat_attention)

    return (yp.reshape(batch, seq, d),
            ys.reshape(dec_batch, dec_seq, d),
            kg.reshape(batch, 1, seq, GQA_KV_HEADS, HEAD_DIM),
            vg.reshape(batch, 1, seq, GQA_KV_HEADS, HEAD_DIM),
            kd.reshape(batch, 1, seq, DIFF_HEADS, 2, HEAD_DIM),
            vd.reshape(batch, 1, seq, DIFF_HEADS, DIFF_V_DIM))
```

```python
import functools
import math

import jax
import jax.numpy as jnp
from jax import lax
from jax.experimental import pallas as pl
from jax.experimental.pallas import tpu as pltpu

F32 = jnp.float32
BF16 = jnp.bfloat16

D_MODEL = 2048
HEAD_DIM = 128
GRID_W = 64
GQA_Q_HEADS = 8
GQA_KV_HEADS = 2
GQA_GROUP = GQA_Q_HEADS // GQA_KV_HEADS
DIFF_HEADS = 4
DIFF_V_DIM = 2 * HEAD_DIM
GQA_Q_W = GQA_Q_HEADS * HEAD_DIM
GQA_KV_W = GQA_KV_HEADS * HEAD_DIM
DIFF_QK_W = DIFF_HEADS * 2 * HEAD_DIM
DIFF_V_W = DIFF_HEADS * DIFF_V_DIM
GATE_W = 2 * D_MODEL
PROJ_W = GQA_Q_W + 2 * GQA_KV_W + 2 * DIFF_QK_W + DIFF_V_W + GATE_W
ROPE_THETA = 10000.0
ROPE_PAIRS_PER_AXIS = HEAD_DIM // 4
PEER_HEADS = 8
PEER_N_KEYS = 128
PEER_N_EXPERTS = PEER_N_KEYS * PEER_N_KEYS
PEER_HALF = 128
PEER_TOPK = 16
NORM_EPS = 1e-6
ATTN_SCALE = HEAD_DIM ** -0.5
LAMBDA_INIT = 0.8 - 0.6 * math.exp(-0.3 * 0)

LANES = 128
SUBLANES = 8
BF16_ROWS = 16
MIB = 1024 * 1024
VMEM_LIMIT = 56 * MIB

INPROJ_TM = 1024
INPROJ_TN = 256
ATTN_TQ = 256
MERGE_TM = 256
SCORE_TB = 256
DENSE_NB = 512
DENSE_EB = 1024
DENSE_PARTS = 4
DENSE_RB = BF16_ROWS
FINAL_TM = 512

_SEG_WIDTHS = (GQA_Q_W, GQA_KV_W, GQA_KV_W, DIFF_QK_W, DIFF_QK_W, DIFF_V_W, GATE_W)


def _params(n_axes):
    return pltpu.CompilerParams(dimension_semantics=("arbitrary",) * n_axes, vmem_limit_bytes=VMEM_LIMIT)


def _rms(x):
    return x * lax.rsqrt(jnp.mean(x * x, axis=-1, keepdims=True) + NORM_EPS)


def _split_bf16(x):
    hi = x.astype(BF16)
    lo = (x - hi.astype(F32)).astype(BF16)
    return hi, lo


def _ada_kernel(cond_ref, w_ref, b_ref, o_ref):
    c = cond_ref[...]
    s = c * jax.nn.sigmoid(c)
    s_hi, s_lo = _split_bf16(s)
    w_hi, w_lo = _split_bf16(w_ref[...])
    acc = jnp.dot(s_hi, w_hi, preferred_element_type=F32)
    acc += jnp.dot(s_hi, w_lo, preferred_element_type=F32)
    acc += jnp.dot(s_lo, w_hi, preferred_element_type=F32)
    o_ref[...] = acc + b_ref[...]


def _ada(cond, w_ada, b_ada):
    rows, d = cond.shape
    n = w_ada.shape[1]
    tn = 1024
    return pl.pallas_call(
        _ada_kernel,
        grid=(n // tn,),
        in_specs=[pl.BlockSpec((rows, d), lambda j: (0, 0)),
                  pl.BlockSpec((d, tn), lambda j: (0, j)),
                  pl.BlockSpec((1, tn), lambda j: (0, j))],
        out_specs=pl.BlockSpec((rows, tn), lambda j: (0, j)),
        out_shape=jax.ShapeDtypeStruct((rows, n), F32),
        compiler_params=_params(1),
        name="ada",
    )(cond, w_ada, b_ada)


def _rope(x, cos, sin_signed, even_chunk):
    up = pltpu.roll(x, HEAD_DIM - HEAD_DIM // 4, axis=1)
    down = pltpu.roll(x, HEAD_DIM // 4, axis=1)
    return x * cos + jnp.where(even_chunk, up, down) * sin_signed


def _inproj_kernel(*refs, rope, seg_starts):
    if rope:
        (x_ref, mod_ref, g1_ref, w_ref, qn_ref, kn_ref, cos_ref, sin_ref,
         qg_ref, kg_ref, vg_ref, qd_ref, kd_ref, vd_ref, gate_ref, h_scr) = refs
    else:
        (x_ref, mod_ref, g1_ref, w_ref, qn_ref, kn_ref,
         qg_ref, kg_ref, vg_ref, qd_ref, kd_ref, vd_ref, gate_ref, h_scr) = refs
    j = pl.program_id(1)

    @pl.when(j == 0)
    def _():
        y = _rms(x_ref[...]) * g1_ref[...]
        h_scr[...] = (y * (1.0 + mod_ref[0, 1:2, :]) + mod_ref[0, 0:1, :]).astype(BF16)

    z = jnp.dot(h_scr[...], w_ref[...], preferred_element_type=F32)
    heads = [z[:, k * HEAD_DIM:(k + 1) * HEAD_DIM] for k in range(INPROJ_TN // HEAD_DIM)]

    if rope:
        cos = cos_ref[...]
        sin_signed = sin_ref[...]
        lane = lax.broadcasted_iota(jnp.int32, cos.shape, 1)
        even_chunk = (lane // (HEAD_DIM // 4)) % 2 == 0
        maybe_rope = lambda t: _rope(t, cos, sin_signed, even_chunk)
    else:
        maybe_rope = lambda t: t

    def store(ref, parts):
        for k, p in enumerate(parts):
            ref[:, k * HEAD_DIM:(k + 1) * HEAD_DIM] = p.astype(ref.dtype)

    def in_seg(s):
        return jnp.logical_and(j >= seg_starts[s], j < seg_starts[s + 1])

    @pl.when(in_seg(0))
    def _():
        store(qg_ref, [maybe_rope(_rms(t) * qn_ref[...]) * ATTN_SCALE for t in heads])

    @pl.when(in_seg(1))
    def _():
        store(kg_ref, [maybe_rope(_rms(t) * kn_ref[...]) for t in heads])

    @pl.when(in_seg(2))
    def _():
        store(vg_ref, heads)

    @pl.when(in_seg(3))
    def _():
        store(qd_ref, [maybe_rope(t) * ATTN_SCALE for t in heads])

    @pl.when(in_seg(4))
    def _():
        store(kd_ref, [maybe_rope(t) for t in heads])

    @pl.when(in_seg(5))
    def _():
        store(vd_ref, heads)

    @pl.when(in_seg(6))
    def _():
        store(gate_ref, [jax.nn.sigmoid(t) for t in heads])


def _inproj(x, mod, mod_row_fn, g1, w_in, qn, kn, rope_tables, kv_dtype):
    n, d = x.shape
    tm, tn = INPROJ_TM, INPROJ_TN
    seg_blocks = [w // tn for w in _SEG_WIDTHS]
    seg_starts = [0]
    for b in seg_blocks:
        seg_starts.append(seg_starts[-1] + b)
    rope = rope_tables is not None

    def seg_spec(s):
        start, nblk = seg_starts[s], seg_blocks[s]
        return pl.BlockSpec((tm, tn), lambda i, j: (i, jnp.clip(j - start, 0, nblk - 1)))

    in_specs = [pl.BlockSpec((tm, d), lambda i, j: (i, 0)),
                pl.BlockSpec((1, 6, d), lambda i, j: (mod_row_fn(i), 0, 0)),
                pl.BlockSpec((1, d), lambda i, j: (0, 0)),
                pl.BlockSpec((d, tn), lambda i, j: (0, j)),
                pl.BlockSpec((1, HEAD_DIM), lambda i, j: (0, 0)),
                pl.BlockSpec((1, HEAD_DIM), lambda i, j: (0, 0))]
    args = [x, mod, g1, w_in, qn, kn]
    if rope:
        cos, sin_signed = rope_tables
        blocks_per_seq = cos.shape[0] // tm
        in_specs += [pl.BlockSpec((tm, HEAD_DIM), lambda i, j: (i % blocks_per_seq, 0))] * 2
        args += [cos, sin_signed]
    out_dtypes = (BF16, kv_dtype, kv_dtype, BF16, kv_dtype, kv_dtype, BF16)
    out_shape = [jax.ShapeDtypeStruct((n, w), dt) for w, dt in zip(_SEG_WIDTHS, out_dtypes)]
    return pl.pallas_call(
        functools.partial(_inproj_kernel, rope=rope, seg_starts=tuple(seg_starts)),
        grid=(n // tm, PROJ_W // tn),
        in_specs=in_specs,
        out_specs=[seg_spec(s) for s in range(len(_SEG_WIDTHS))],
        out_shape=out_shape,
        scratch_shapes=[pltpu.VMEM((tm, d), BF16)],
        compiler_params=_params(2),
        name="inproj_rope" if rope else "inproj",
    )(*args)


def _nt_dot(a, b):
    return lax.dot_general(a, b, (((1,), (1,)), ((), ())), preferred_element_type=F32)


def _softmax_parts(scores):
    m = scores[0].max(axis=-1, keepdims=True)
    for s in scores[1:]:
        m = jnp.maximum(m, s.max(axis=-1, keepdims=True))
    exps = [jnp.exp(s - m) for s in scores]
    total = exps[0].sum(axis=-1, keepdims=True)
    for e in exps[1:]:
        total = total + e.sum(axis=-1, keepdims=True)
    return exps, 1.0 / total


def _subln(o, g):
    return (_rms(o) * g) * (1.0 - LAMBDA_INIT)


def _ctx_attn_kernel(lam_ref, qg_ref, kg_ref, vg_ref, qd_ref, kd_ref, vd_ref, subg_ref, og_ref, od_ref):
    lam = lam_ref[0]
    for kvh in range(GQA_KV_HEADS):
        cols = slice(kvh * HEAD_DIM, (kvh + 1) * HEAD_DIM)
        k = kg_ref[:, cols].astype(BF16)
        v = vg_ref[:, cols].astype(BF16)
        for g in range(GQA_GROUP):
            hc = slice((kvh * GQA_GROUP + g) * HEAD_DIM, (kvh * GQA_GROUP + g + 1) * HEAD_DIM)
            (e,), inv = _softmax_parts([_nt_dot(qg_ref[:, hc], k)])
            o = jnp.dot(e.astype(BF16), v, preferred_element_type=F32) * inv
            og_ref[:, hc] = o.astype(og_ref.dtype)
    for h in range(DIFF_HEADS):
        c1 = slice((2 * h) * HEAD_DIM, (2 * h + 1) * HEAD_DIM)
        c2 = slice((2 * h + 1) * HEAD_DIM, (2 * h + 2) * HEAD_DIM)
        vc = slice(h * DIFF_V_DIM, (h + 1) * DIFF_V_DIM)
        (e1,), inv1 = _softmax_parts([_nt_dot(qd_ref[:, c1], kd_ref[:, c1].astype(BF16))])
        (e2,), inv2 = _softmax_parts([_nt_dot(qd_ref[:, c2], kd_ref[:, c2].astype(BF16))])
        a = e1 * inv1 - (lam * inv2) * e2
        o = jnp.dot(a.astype(BF16), vd_ref[:, vc].astype(BF16), preferred_element_type=F32)
        od_ref[:, vc] = _subln(o, subg_ref[...]).astype(od_ref.dtype)


def _ctx_attention(lam, qg, kg, vg, qd, kd, vd, subg, seq):
    n = qg.shape[0]
    row = lambda w: pl.BlockSpec((seq, w), lambda b: (b, 0))
    return pl.pallas_call(
        _ctx_attn_kernel,
        grid=(n // seq,),
        in_specs=[pl.BlockSpec(memory_space=pltpu.SMEM),
                  row(GQA_Q_W), row(GQA_KV_W), row(GQA_KV_W), row(DIFF_QK_W), row(DIFF_QK_W), row(DIFF_V_W),
                  pl.BlockSpec((1, DIFF_V_DIM), lambda b: (0, 0))],
        out_specs=[row(GQA_Q_W), row(DIFF_V_W)],
        out_shape=[jax.ShapeDtypeStruct((n, GQA_Q_W), BF16), jax.ShapeDtypeStruct((n, DIFF_V_W), BF16)],
        compiler_params=_params(1),
        name="ctx_attn",
    )(lam, qg, kg, vg, qd, kd, vd, subg)


def _lat_gqa_kernel(q_ref, k_ref, v_ref, ck_ref, cv_ref, o_ref):
    k = k_ref[...]
    v = v_ref[...]
    ck = ck_ref[...].astype(BF16)
    cv = cv_ref[...].astype(BF16)
    for g in range(GQA_GROUP):
        hc = slice(g * HEAD_DIM, (g + 1) * HEAD_DIM)
        q = q_ref[:, hc]
        (e_l, e_c), inv = _softmax_parts([_nt_dot(q, k), _nt_dot(q, ck)])
        o = jnp.dot(e_l.astype(BF16), v, preferred_element_type=F32)
        o += jnp.dot(e_c.astype(BF16), cv, preferred_element_type=F32)
        o_ref[:, hc] = (o * inv).astype(o_ref.dtype)


def _lat_gqa_attention(qg, kg, vg, cache_k, cache_v, n_batch, seq):
    tq = ATTN_TQ
    nq = seq // tq
    past = cache_k.shape[1]
    gw = GQA_GROUP * HEAD_DIM
    return pl.pallas_call(
        _lat_gqa_kernel,
        grid=(n_batch, GQA_KV_HEADS, nq),
        in_specs=[pl.BlockSpec((tq, gw), lambda b, h, q: (b * nq + q, h)),
                  pl.BlockSpec((seq, HEAD_DIM), lambda b, h, q: (b, h)),
                  pl.BlockSpec((seq, HEAD_DIM), lambda b, h, q: (b, h)),
                  pl.BlockSpec((None, past, HEAD_DIM), lambda b, h, q: (b, 0, h)),
                  pl.BlockSpec((None, past, HEAD_DIM), lambda b, h, q: (b, 0, h))],
        out_specs=pl.BlockSpec((tq, gw), lambda b, h, q: (b * nq + q, h)),
        out_shape=jax.ShapeDtypeStruct(qg.shape, BF16),
        compiler_params=_params(3),
        name="lat_gqa",
    )(qg, kg, vg, cache_k, cache_v)


def _lat_diff_kernel(lam_ref, q_ref, k_ref, v_ref, ck_ref, cv_ref, subg_ref, o_ref):
    lam = lam_ref[0]
    c1 = slice(0, HEAD_DIM)
    c2 = slice(HEAD_DIM, 2 * HEAD_DIM)
    ck = ck_ref[...].astype(BF16)
    (e1_l, e1_c), inv1 = _softmax_parts([_nt_dot(q_ref[:, c1], k_ref[:, c1]), _nt_dot(q_ref[:, c1], ck[:, c1])])
    (e2_l, e2_c), inv2 = _softmax_parts([_nt_dot(q_ref[:, c2], k_ref[:, c2]), _nt_dot(q_ref[:, c2], ck[:, c2])])
    w2 = lam * inv2
    a_l = e1_l * inv1 - w2 * e2_l
    a_c = e1_c * inv1 - w2 * e2_c
    o = jnp.dot(a_l.astype(BF16), v_ref[...], preferred_element_type=F32)
    o += jnp.dot(a_c.astype(BF16), cv_ref[...].astype(BF16), preferred_element_type=F32)
    o_ref[...] = _subln(o, subg_ref[...]).astype(o_ref.dtype)


def _lat_diff_attention(lam, qd, kd, vd, cache_k, cache_v, subg, n_batch, seq):
    tq = ATTN_TQ
    nq = seq // tq
    past = cache_k.shape[1]
    hw = 2 * HEAD_DIM
    return pl.pallas_call(
        _lat_diff_kernel,
        grid=(n_batch, DIFF_HEADS, nq),
        in_specs=[pl.BlockSpec(memory_space=pltpu.SMEM),
                  pl.BlockSpec((tq, hw), lambda b, h, q: (b * nq + q, h)),
                  pl.BlockSpec((seq, hw), lambda b, h, q: (b, h)),
                  pl.BlockSpec((seq, DIFF_V_DIM), lambda b, h, q: (b, h)),
                  pl.BlockSpec((None, past, hw), lambda b, h, q: (b, 0, h)),
                  pl.BlockSpec((None, past, DIFF_V_DIM), lambda b, h, q: (b, 0, h)),
                  pl.BlockSpec((1, DIFF_V_DIM), lambda b, h, q: (0, 0))],
        out_specs=pl.BlockSpec((tq, DIFF_V_DIM), lambda b, h, q: (b * nq + q, h)),
        out_shape=jax.ShapeDtypeStruct(qd.shape, BF16),
        compiler_params=_params(3),
        name="lat_diff",
    )(lam, qd, kd, vd, cache_k, cache_v, subg)


def _merge_kernel(x_ref, og_ref, od_ref, gate_ref, wa_ref, wb_ref, wo_ref, mod_ref, g2_ref, x1_ref, h2t_ref):
    ya = jnp.dot(og_ref[...], wa_ref[...], preferred_element_type=F32)
    yb = jnp.dot(od_ref[...], wb_ref[...], preferred_element_type=F32)
    merged = gate_ref[:, :D_MODEL].astype(F32) * ya + gate_ref[:, D_MODEL:].astype(F32) * yb
    y = jnp.dot(merged.astype(BF16), wo_ref[...], preferred_element_type=F32)
    x1 = x_ref[...] + mod_ref[0, 2:3, :] * y
    x1_ref[...] = x1
    h2 = (_rms(x1) * g2_ref[...]) * (1.0 + mod_ref[0, 4:5, :]) + mod_ref[0, 3:4, :]
    h2t_ref[...] = h2.T.astype(BF16)


def _merge(x, og, od, gates, wa, wb, wo, mod, mod_row_fn, g2):
    n, d = x.shape
    tm = MERGE_TM
    resident = lambda shape: pl.BlockSpec(shape, lambda i: (0, 0), pipeline_mode=pl.Buffered(1))
    return pl.pallas_call(
        _merge_kernel,
        grid=(n // tm,),
        in_specs=[pl.BlockSpec((tm, d), lambda i: (i, 0)),
                  pl.BlockSpec((tm, GQA_Q_W), lambda i: (i, 0)),
                  pl.BlockSpec((tm, DIFF_V_W), lambda i: (i, 0)),
                  pl.BlockSpec((tm, GATE_W), lambda i: (i, 0)),
                  resident(wa.shape), resident(wb.shape), resident(wo.shape),
                  pl.BlockSpec((1, 6, d), lambda i: (mod_row_fn(i), 0, 0)),
                  pl.BlockSpec((1, d), lambda i: (0, 0))],
        out_specs=[pl.BlockSpec((tm, d), lambda i: (i, 0)),
                   pl.BlockSpec((d, tm), lambda i: (0, i))],
        out_shape=[jax.ShapeDtypeStruct((n, d), F32), jax.ShapeDtypeStruct((d, n), BF16)],
        compiler_params=_params(1),
        name="merge",
    )(x, og, od, gates, wa, wb, wo, mod, g2)


def _top_values(s, k):
    rows = s.shape[0]
    iota = lax.broadcasted_iota(jnp.int32, s.shape, 0).astype(F32)
    vals = []
    for r in range(k):
        m = jnp.max(s, axis=0, keepdims=True)
        vals.append(m)
        if r + 1 < k:
            first = jnp.min(jnp.where(s == m, iota, float(rows)), axis=0, keepdims=True)
            s = jnp.where(iota == first, -jnp.inf, s)
    return jnp.concatenate(vals, axis=0)


def _peer_score_kernel(h2t_ref, wqt_ref, keys_ref, s_ref, sel_ref):
    qt = jnp.dot(wqt_ref[...], h2t_ref[...], preferred_element_type=F32).astype(BF16)
    taus = []
    shifts = []
    for h in range(PEER_HEADS):
        r1 = slice((2 * h) * PEER_N_KEYS, (2 * h + 1) * PEER_N_KEYS)
        r2 = slice((2 * h + 1) * PEER_N_KEYS, (2 * h + 2) * PEER_N_KEYS)
        s1 = jnp.dot(keys_ref[2 * h], qt[r1], preferred_element_type=F32)
        s2 = jnp.dot(keys_ref[2 * h + 1], qt[r2], preferred_element_type=F32)
        top1 = _top_values(s1, PEER_TOPK)
        top2 = _top_values(s2, PEER_TOPK)
        cand = jnp.concatenate([top1[r:r + 1] + top2 for r in range(PEER_TOPK)], axis=0)
        tops = _top_values(cand, PEER_TOPK)
        z = jnp.sum(jnp.exp(tops - tops[0:1]), axis=0, keepdims=True)
        s_ref[r1, :] = s1
        s_ref[r2, :] = s2
        taus.append(tops[PEER_TOPK - 1:PEER_TOPK])
        shifts.append(tops[0:1] + jnp.log(z))
    sel_ref[...] = jnp.concatenate(taus + shifts, axis=0)


def _peer_scores(h2t, wqt, keys):
    d, n = h2t.shape
    tb = SCORE_TB
    rows = PEER_HEADS * 2 * PEER_N_KEYS
    return pl.pallas_call(
        _peer_score_kernel,
        grid=(n // tb,),
        in_specs=[pl.BlockSpec((d, tb), lambda i: (0, i)),
                  pl.BlockSpec(wqt.shape, lambda i: (0, 0), pipeline_mode=pl.Buffered(1)),
                  pl.BlockSpec(keys.shape, lambda i: (0, 0, 0), pipeline_mode=pl.Buffered(1))],
        out_specs=[pl.BlockSpec((rows, tb), lambda i: (0, i)),
                   pl.BlockSpec((2 * PEER_HEADS, tb), lambda i: (0, i))],
        out_shape=[jax.ShapeDtypeStruct((rows, n), F32),
                   jax.ShapeDtypeStruct((2 * PEER_HEADS, n), F32)],
        compiler_params=_params(1),
        name="peer_score",
    )(h2t, wqt, keys)


def _expert_coefs(at_ref, coef_ref, s_ref, sel_ref, key_group, first_row):
    eb, nb = at_ref.shape
    for al in range(eb // PEER_N_KEYS):
        row = first_row + al
        for t in range(nb // LANES):
            ls = slice(t * LANES, (t + 1) * LANES)
            pick = lax.broadcasted_iota(jnp.int32, (SUBLANES, LANES), 0) == row
            s1b = [jnp.sum(jnp.where(pick, s_ref[pl.ds((2 * h) * PEER_N_KEYS + key_group, SUBLANES), ls], 0.0),
                           axis=0, keepdims=True) for h in range(PEER_HEADS)]
            taub = [sel_ref[h:h + 1, ls] for h in range(PEER_HEADS)]
            s1c = [s1b[h] - sel_ref[PEER_HEADS + h:PEER_HEADS + h + 1, ls] for h in range(PEER_HEADS)]
            for c in range(PEER_N_KEYS // DENSE_RB):
                w = jnp.zeros((DENSE_RB, LANES), F32)
                for h in range(PEER_HEADS):
                    rows = slice((2 * h + 1) * PEER_N_KEYS + c * DENSE_RB, (2 * h + 1) * PEER_N_KEYS + (c + 1) * DENSE_RB)
                    s2 = s_ref[rows, ls]
                    w = w + jnp.where(s1b[h] + s2 >= taub[h], jnp.exp(s1c[h] + s2), 0.0)
                out_rows = slice(al * PEER_N_KEYS + c * DENSE_RB, al * PEER_N_KEYS + (c + 1) * DENSE_RB)
                pre = at_ref[out_rows, ls]
                act = 0.5 * pre * (1.0 + lax.erf(pre * math.sqrt(0.5)))
                coef_ref[out_rows, ls] = (act * w).astype(BF16)


def _peer_dense_kernel(h2t_ref, u_ref, vt_ref, s_ref, sel_ref, acc_ref, *part_refs):
    n_parts = len(part_refs) // 2
    at_refs, coef_refs = part_refs[:n_parts], part_refs[n_parts:]
    step = pl.program_id(1)
    part = at_refs[0].shape[0]
    keys_per_part = part // PEER_N_KEYS
    assert n_parts * keys_per_part == SUBLANES

    @pl.when(step == 0)
    def _():
        acc_ref[...] = jnp.zeros_like(acc_ref)

    key_group = pl.multiple_of(step * SUBLANES, SUBLANES)

    def project(q):
        at_refs[q][...] = jnp.dot(u_ref[q * part:(q + 1) * part, :], h2t_ref[...], preferred_element_type=F32)

    project(0)
    for q in range(n_parts):
        if q + 1 < n_parts:
            project(q + 1)
        _expert_coefs(at_refs[q], coef_refs[q], s_ref, sel_ref, key_group, q * keys_per_part)
        acc_ref[...] += jnp.dot(vt_ref[:, q * part:(q + 1) * part], coef_refs[q][...], preferred_element_type=F32)


def _peer_dense(h2t, u, vt, s, sel):
    d, n = h2t.shape
    nb, eb = DENSE_NB, DENSE_EB
    part = eb // DENSE_PARTS
    rows = s.shape[0]
    return pl.pallas_call(
        _peer_dense_kernel,
        grid=(n // nb, PEER_N_EXPERTS // eb),
        in_specs=[pl.BlockSpec((d, nb), lambda i, j: (0, i)),
                  pl.BlockSpec((eb, d), lambda i, j: (j, 0)),
                  pl.BlockSpec((d, eb), lambda i, j: (0, j)),
                  pl.BlockSpec((rows, nb), lambda i, j: (0, i)),
                  pl.BlockSpec((2 * PEER_HEADS, nb), lambda i, j: (0, i))],
        out_specs=pl.BlockSpec((d, nb), lambda i, j: (0, i)),
        out_shape=jax.ShapeDtypeStruct((d, n), F32),
        scratch_shapes=([pltpu.VMEM((part, nb), F32)] * DENSE_PARTS + [pltpu.VMEM((part, nb), BF16)] * DENSE_PARTS),
        compiler_params=_params(2),
        name="peer_dense",
    )(h2t, u, vt, s, sel)


def _final_kernel(x1_ref, pt_ref, mod_ref, g_ref, o_ref):
    x2 = x1_ref[...] + mod_ref[0, 5:6, :] * pt_ref[...].T
    o_ref[...] = _rms(x2) * g_ref[...]


def _final(x1, peer_t, mod, mod_row_fn, g):
    n, d = x1.shape
    tm = FINAL_TM
    return pl.pallas_call(
        _final_kernel,
        grid=(n // tm,),
        in_specs=[pl.BlockSpec((tm, d), lambda i: (i, 0)),
                  pl.BlockSpec((d, tm), lambda i: (0, i)),
                  pl.BlockSpec((1, 6, d), lambda i: (mod_row_fn(i), 0, 0)),
                  pl.BlockSpec((1, d), lambda i: (0, 0))],
        out_specs=pl.BlockSpec((tm, d), lambda i: (i, 0)),
        out_shape=jax.ShapeDtypeStruct((n, d), F32),
        compiler_params=_params(1),
        name="final",
    )(x1, peer_t, mod, g)


def _rope_tables(n_tokens):
    rows = n_tokens // GRID_W
    row = jnp.repeat(jnp.arange(rows, dtype=F32), GRID_W)
    col = jnp.tile(jnp.arange(GRID_W, dtype=F32), rows)
    inv = ROPE_THETA ** (-jnp.arange(ROPE_PAIRS_PER_AXIS, dtype=F32) / ROPE_PAIRS_PER_AXIS)
    ar = row[:, None] * inv
    ac = col[:, None] * inv
    ang = jnp.concatenate([ar, ar, ac, ac], axis=-1)
    sign = jnp.where((jnp.arange(HEAD_DIM) // (HEAD_DIM // 4)) % 2 == 0, -1.0, 1.0).astype(F32)
    return jnp.cos(ang), jnp.sin(ang) * sign


def _trunk(x, mod, rows_per_mod, first_mod_row, w, rope_tables, attention):
    def mod_row_for(tm):
        return lambda i: first_mod_row + (i * tm) // rows_per_mod

    kv_dtype = BF16 if rope_tables is not None else F32
    qg, kg, vg, qd, kd, vd, gates = _inproj(x, mod, mod_row_for(INPROJ_TM), w["norm1_g"], w["w_in"],
                                            w["q_norm_g"], w["k_norm_g"], rope_tables, kv_dtype)
    og, od = attention(qg, kg, vg, qd, kd, vd)
    x1, h2t = _merge(x, og, od, gates, w["w_a"], w["w_b"], w["w_o"], mod, mod_row_for(MERGE_TM), w["norm2_g"])
    s, sel = _peer_scores(h2t, w["wq_t"], w["keys"])
    peer_t = _peer_dense(h2t, w["u"], w["v_t"], s, sel)
    y = _final(x1, peer_t, mod, mod_row_for(FINAL_TM), w["final_g"])
    return y, (kg, vg, kd, vd)


def kernel(x_prompt, x_sample, c, cache_gqa_k, cache_gqa_v, cache_diff_k, cache_diff_v, c_ctx, w_ada, b_ada, norm1_g, norm2_g, w_in, q_norm_g, k_norm_g, lambda_q1, lambda_k1, lambda_q2, lambda_k2, diff_subln_g, w_branch_a, w_branch_b, w_out, peer_w_query, peer_sub_keys, peer_u, peer_v, final_norm_g):
    assert w_in.shape[0] == 1, "one trunk layer"
    batch, seq, d = x_prompt.shape
    dec_batch, dec_seq, _ = x_sample.shape
    past = cache_gqa_k.shape[2]
    l = 0

    w = {
        "norm1_g": norm1_g[l][None, :], "norm2_g": norm2_g[l][None, :], "final_g": final_norm_g[None, :],
        "q_norm_g": q_norm_g[l][None, :], "k_norm_g": k_norm_g[l][None, :],
        "w_in": w_in[l].astype(BF16),
        "w_a": w_branch_a[l].astype(BF16), "w_b": w_branch_b[l].astype(BF16), "w_o": w_out[l].astype(BF16),
        "wq_t": peer_w_query[l].T.astype(BF16),
        "keys": peer_sub_keys[l].reshape(PEER_HEADS * 2, PEER_N_KEYS, PEER_HALF).astype(BF16),
        "u": peer_u[l].astype(BF16),
        "v_t": peer_v[l].T.astype(BF16),
    }
    subg = diff_subln_g[l][None, :]
    lam = (jnp.exp(jnp.sum((lambda_q1[l] * lambda_k1[l]).astype(F32)))
           - jnp.exp(jnp.sum((lambda_q2[l] * lambda_k2[l]).astype(F32))) + LAMBDA_INIT).reshape(1)

    cond = jnp.concatenate([c_ctx[None, :], c], axis=0)
    n_cond = cond.shape[0]
    cond = jnp.pad(cond, ((0, SUBLANES - n_cond), (0, 0)))
    mod = _ada(cond, w_ada[l], b_ada[l][None, :])[:n_cond].reshape(n_cond, 6, d)

    def ctx_attention(qg, kg, vg, qd, kd, vd):
        return _ctx_attention(lam, qg, kg, vg, qd, kd, vd, subg, seq)

    ck_g = cache_gqa_k[:, l].reshape(dec_batch, past, GQA_KV_W)
    cv_g = cache_gqa_v[:, l].reshape(dec_batch, past, GQA_KV_W)
    ck_d = cache_diff_k[:, l].reshape(dec_batch, past, DIFF_QK_W)
    cv_d = cache_diff_v[:, l].reshape(dec_batch, past, DIFF_V_W)

    def lat_attention(qg, kg, vg, qd, kd, vd):
        og = _lat_gqa_attention(qg, kg, vg, ck_g, cv_g, dec_batch, dec_seq)
        od = _lat_diff_attention(lam, qd, kd, vd, ck_d, cv_d, subg, dec_batch, dec_seq)
        return og, od

    n_ctx = batch * seq
    yp, (kg, vg, kd, vd) = _trunk(x_prompt.reshape(n_ctx, d), mod, n_ctx, 0, w, None, ctx_attention)
    ys, _ = _trunk(x_sample.reshape(dec_batch * dec_seq, d), mod, dec_seq, 1, w,
                   _rope_tables(dec_seq), lat_attention)

    return (yp.reshape(batch, seq, d),
            ys.reshape(dec_batch, dec_seq, d),
            kg.reshape(batch, 1, seq, GQA_KV_HEADS, HEAD_DIM),
            vg.reshape(batch, 1, seq, GQA_KV_HEADS, HEAD_DIM),
            kd.reshape(batch, 1, seq, DIFF_HEADS, 2, HEAD_DIM),
            vd.reshape(batch, 1, seq, DIFF_HEADS, DIFF_V_DIM))
```

```python
import functools
import math

import jax
import jax.numpy as jnp
from jax import lax
from jax.experimental import pallas as pl
from jax.experimental.pallas import tpu as pltpu

F32 = jnp.float32
BF16 = jnp.bfloat16

D_MODEL = 2048
HEAD_DIM = 128
GRID_W = 64
GQA_Q_HEADS = 8
GQA_KV_HEADS = 2
GQA_GROUP = GQA_Q_HEADS // GQA_KV_HEADS
DIFF_HEADS = 4
DIFF_V_DIM = 2 * HEAD_DIM
GQA_Q_W = GQA_Q_HEADS * HEAD_DIM
GQA_KV_W = GQA_KV_HEADS * HEAD_DIM
DIFF_QK_W = DIFF_HEADS * 2 * HEAD_DIM
DIFF_V_W = DIFF_HEADS * DIFF_V_DIM
GATE_W = 2 * D_MODEL
PROJ_W = GQA_Q_W + 2 * GQA_KV_W + 2 * DIFF_QK_W + DIFF_V_W + GATE_W
ROPE_THETA = 10000.0
ROPE_PAIRS_PER_AXIS = HEAD_DIM // 4
PEER_HEADS = 8
PEER_N_KEYS = 128
PEER_N_EXPERTS = PEER_N_KEYS * PEER_N_KEYS
PEER_HALF = 128
PEER_TOPK = 16
NORM_EPS = 1e-6
ATTN_SCALE = HEAD_DIM ** -0.5
LAMBDA_INIT = 0.8 - 0.6 * math.exp(-0.3 * 0)

LANES = 128
SUBLANES = 8
BF16_ROWS = 16
MIB = 1024 * 1024
VMEM_LIMIT = 56 * MIB

INPROJ_TM = 1024
INPROJ_TN = 256
ATTN_TQ = 256
MERGE_TM = 256
SCORE_TB = 256
DENSE_NB = 512
DENSE_EB = 1024
DENSE_RB = BF16_ROWS
FINAL_TM = 512

_SEG_WIDTHS = (GQA_Q_W, GQA_KV_W, GQA_KV_W, DIFF_QK_W, DIFF_QK_W, DIFF_V_W, GATE_W)


def _params(n_axes):
    return pltpu.CompilerParams(dimension_semantics=("arbitrary",) * n_axes, vmem_limit_bytes=VMEM_LIMIT)


def _rms(x):
    return x * lax.rsqrt(jnp.mean(x * x, axis=-1, keepdims=True) + NORM_EPS)


def _split_bf16(x):
    hi = x.astype(BF16)
    lo = (x - hi.astype(F32)).astype(BF16)
    return hi, lo


def _ada_kernel(cond_ref, w_ref, b_ref, o_ref):
    c = cond_ref[...]
    s = c * jax.nn.sigmoid(c)
    s_hi, s_lo = _split_bf16(s)
    w_hi, w_lo = _split_bf16(w_ref[...])
    acc = jnp.dot(s_hi, w_hi, preferred_element_type=F32)
    acc += jnp.dot(s_hi, w_lo, preferred_element_type=F32)
    acc += jnp.dot(s_lo, w_hi, preferred_element_type=F32)
    o_ref[...] = acc + b_ref[...]


def _ada(cond, w_ada, b_ada):
    rows, d = cond.shape
    n = w_ada.shape[1]
    tn = 1024
    return pl.pallas_call(
        _ada_kernel,
        grid=(n // tn,),
        in_specs=[pl.BlockSpec((rows, d), lambda j: (0, 0)),
                  pl.BlockSpec((d, tn), lambda j: (0, j)),
                  pl.BlockSpec((1, tn), lambda j: (0, j))],
        out_specs=pl.BlockSpec((rows, tn), lambda j: (0, j)),
        out_shape=jax.ShapeDtypeStruct((rows, n), F32),
        compiler_params=_params(1),
        name="ada",
    )(cond, w_ada, b_ada)


def _rope(x, cos, sin_signed, even_chunk):
    up = pltpu.roll(x, HEAD_DIM - HEAD_DIM // 4, axis=1)
    down = pltpu.roll(x, HEAD_DIM // 4, axis=1)
    return x * cos + jnp.where(even_chunk, up, down) * sin_signed


def _inproj_kernel(*refs, rope, seg_starts):
    if rope:
        (x_ref, mod_ref, g1_ref, w_ref, qn_ref, kn_ref, cos_ref, sin_ref,
         qg_ref, kg_ref, vg_ref, qd_ref, kd_ref, vd_ref, gate_ref, h_scr) = refs
    else:
        (x_ref, mod_ref, g1_ref, w_ref, qn_ref, kn_ref,
         qg_ref, kg_ref, vg_ref, qd_ref, kd_ref, vd_ref, gate_ref, h_scr) = refs
    j = pl.program_id(1)

    @pl.when(j == 0)
    def _():
        y = _rms(x_ref[...]) * g1_ref[...]
        h_scr[...] = (y * (1.0 + mod_ref[0, 1:2, :]) + mod_ref[0, 0:1, :]).astype(BF16)

    z = jnp.dot(h_scr[...], w_ref[...], preferred_element_type=F32)
    heads = [z[:, k * HEAD_DIM:(k + 1) * HEAD_DIM] for k in range(INPROJ_TN // HEAD_DIM)]

    if rope:
        cos = cos_ref[...]
        sin_signed = sin_ref[...]
        lane = lax.broadcasted_iota(jnp.int32, cos.shape, 1)
        even_chunk = (lane // (HEAD_DIM // 4)) % 2 == 0
        maybe_rope = lambda t: _rope(t, cos, sin_signed, even_chunk)
    else:
        maybe_rope = lambda t: t

    def store(ref, parts):
        for k, p in enumerate(parts):
            ref[:, k * HEAD_DIM:(k + 1) * HEAD_DIM] = p.astype(ref.dtype)

    def in_seg(s):
        return jnp.logical_and(j >= seg_starts[s], j < seg_starts[s + 1])

    @pl.when(in_seg(0))
    def _():
        store(qg_ref, [maybe_rope(_rms(t) * qn_ref[...]) * ATTN_SCALE for t in heads])

    @pl.when(in_seg(1))
    def _():
        store(kg_ref, [maybe_rope(_rms(t) * kn_ref[...]) for t in heads])

    @pl.when(in_seg(2))
    def _():
        store(vg_ref, heads)

    @pl.when(in_seg(3))
    def _():
        store(qd_ref, [maybe_rope(t) * ATTN_SCALE for t in heads])

    @pl.when(in_seg(4))
    def _():
        store(kd_ref, [maybe_rope(t) for t in heads])

    @pl.when(in_seg(5))
    def _():
        store(vd_ref, heads)

    @pl.when(in_seg(6))
    def _():
        store(gate_ref, [jax.nn.sigmoid(t) for t in heads])


def _inproj(x, mod, mod_row_fn, g1, w_in, qn, kn, rope_tables, kv_dtype):
    n, d = x.shape
    tm, tn = INPROJ_TM, INPROJ_TN
    seg_blocks = [w // tn for w in _SEG_WIDTHS]
    seg_starts = [0]
    for b in seg_blocks:
        seg_starts.append(seg_starts[-1] + b)
    rope = rope_tables is not None

    def seg_spec(s):
        start, nblk = seg_starts[s], seg_blocks[s]
        return pl.BlockSpec((tm, tn), lambda i, j: (i, jnp.clip(j - start, 0, nblk - 1)))

    in_specs = [pl.BlockSpec((tm, d), lambda i, j: (i, 0)),
                pl.BlockSpec((1, 6, d), lambda i, j: (mod_row_fn(i), 0, 0)),
                pl.BlockSpec((1, d), lambda i, j: (0, 0)),
                pl.BlockSpec((d, tn), lambda i, j: (0, j)),
                pl.BlockSpec((1, HEAD_DIM), lambda i, j: (0, 0)),
                pl.BlockSpec((1, HEAD_DIM), lambda i, j: (0, 0))]
    args = [x, mod, g1, w_in, qn, kn]
    if rope:
        cos, sin_signed = rope_tables
        blocks_per_seq = cos.shape[0] // tm
        in_specs += [pl.BlockSpec((tm, HEAD_DIM), lambda i, j: (i % blocks_per_seq, 0))] * 2
        args += [cos, sin_signed]
    out_dtypes = (BF16, kv_dtype, kv_dtype, BF16, kv_dtype, kv_dtype, BF16)
    out_shape = [jax.ShapeDtypeStruct((n, w), dt) for w, dt in zip(_SEG_WIDTHS, out_dtypes)]
    return pl.pallas_call(
        functools.partial(_inproj_kernel, rope=rope, seg_starts=tuple(seg_starts)),
        grid=(n // tm, PROJ_W // tn),
        in_specs=in_specs,
        out_specs=[seg_spec(s) for s in range(len(_SEG_WIDTHS))],
        out_shape=out_shape,
        scratch_shapes=[pltpu.VMEM((tm, d), BF16)],
        compiler_params=_params(2),
        name="inproj_rope" if rope else "inproj",
    )(*args)


def _nt_dot(a, b):
    return lax.dot_general(a, b, (((1,), (1,)), ((), ())), preferred_element_type=F32)


def _softmax_parts(scores):
    m = scores[0].max(axis=-1, keepdims=True)
    for s in scores[1:]:
        m = jnp.maximum(m, s.max(axis=-1, keepdims=True))
    exps = [jnp.exp(s - m) for s in scores]
    total = exps[0].sum(axis=-1, keepdims=True)
    for e in exps[1:]:
        total = total + e.sum(axis=-1, keepdims=True)
    return exps, 1.0 / total


def _subln(o, g):
    return (_rms(o) * g) * (1.0 - LAMBDA_INIT)


def _ctx_attn_kernel(lam_ref, qg_ref, kg_ref, vg_ref, qd_ref, kd_ref, vd_ref, subg_ref, og_ref, od_ref):
    lam = lam_ref[0]
    for kvh in range(GQA_KV_HEADS):
        cols = slice(kvh * HEAD_DIM, (kvh + 1) * HEAD_DIM)
        k = kg_ref[:, cols].astype(BF16)
        v = vg_ref[:, cols].astype(BF16)
        for g in range(GQA_GROUP):
            hc = slice((kvh * GQA_GROUP + g) * HEAD_DIM, (kvh * GQA_GROUP + g + 1) * HEAD_DIM)
            (e,), inv = _softmax_parts([_nt_dot(qg_ref[:, hc], k)])
            o = jnp.dot(e.astype(BF16), v, preferred_element_type=F32) * inv
            og_ref[:, hc] = o.astype(og_ref.dtype)
    for h in range(DIFF_HEADS):
        c1 = slice((2 * h) * HEAD_DIM, (2 * h + 1) * HEAD_DIM)
        c2 = slice((2 * h + 1) * HEAD_DIM, (2 * h + 2) * HEAD_DIM)
        vc = slice(h * DIFF_V_DIM, (h + 1) * DIFF_V_DIM)
        (e1,), inv1 = _softmax_parts([_nt_dot(qd_ref[:, c1], kd_ref[:, c1].astype(BF16))])
        (e2,), inv2 = _softmax_parts([_nt_dot(qd_ref[:, c2], kd_ref[:, c2].astype(BF16))])
        a = e1 * inv1 - (lam * inv2) * e2
        o = jnp.dot(a.astype(BF16), vd_ref[:, vc].astype(BF16), preferred_element_type=F32)
        od_ref[:, vc] = _subln(o, subg_ref[...]).astype(od_ref.dtype)


def _ctx_attention(lam, qg, kg, vg, qd, kd, vd, subg, seq):
    n = qg.shape[0]
    row = lambda w: pl.BlockSpec((seq, w), lambda b: (b, 0))
    return pl.pallas_call(
        _ctx_attn_kernel,
        grid=(n // seq,),
        in_specs=[pl.BlockSpec(memory_space=pltpu.SMEM),
                  row(GQA_Q_W), row(GQA_KV_W), row(GQA_KV_W), row(DIFF_QK_W), row(DIFF_QK_W), row(DIFF_V_W),
                  pl.BlockSpec((1, DIFF_V_DIM), lambda b: (0, 0))],
        out_specs=[row(GQA_Q_W), row(DIFF_V_W)],
        out_shape=[jax.ShapeDtypeStruct((n, GQA_Q_W), BF16), jax.ShapeDtypeStruct((n, DIFF_V_W), BF16)],
        compiler_params=_params(1),
        name="ctx_attn",
    )(lam, qg, kg, vg, qd, kd, vd, subg)


def _lat_gqa_kernel(q_ref, k_ref, v_ref, ck_ref, cv_ref, o_ref):
    k = k_ref[...]
    v = v_ref[...]
    ck = ck_ref[...].astype(BF16)
    cv = cv_ref[...].astype(BF16)
    for g in range(GQA_GROUP):
        hc = slice(g * HEAD_DIM, (g + 1) * HEAD_DIM)
        q = q_ref[:, hc]
        (e_l, e_c), inv = _softmax_parts([_nt_dot(q, k), _nt_dot(q, ck)])
        o = jnp.dot(e_l.astype(BF16), v, preferred_element_type=F32)
        o += jnp.dot(e_c.astype(BF16), cv, preferred_element_type=F32)
        o_ref[:, hc] = (o * inv).astype(o_ref.dtype)


def _lat_gqa_attention(qg, kg, vg, cache_k, cache_v, n_batch, seq):
    tq = ATTN_TQ
    nq = seq // tq
    past = cache_k.shape[1]
    gw = GQA_GROUP * HEAD_DIM
    return pl.pallas_call(
        _lat_gqa_kernel,
        grid=(n_batch, GQA_KV_HEADS, nq),
        in_specs=[pl.BlockSpec((tq, gw), lambda b, h, q: (b * nq + q, h)),
                  pl.BlockSpec((seq, HEAD_DIM), lambda b, h, q: (b, h)),
                  pl.BlockSpec((seq, HEAD_DIM), lambda b, h, q: (b, h)),
                  pl.BlockSpec((None, past, HEAD_DIM), lambda b, h, q: (b, 0, h)),
                  pl.BlockSpec((None, past, HEAD_DIM), lambda b, h, q: (b, 0, h))],
        out_specs=pl.BlockSpec((tq, gw), lambda b, h, q: (b * nq + q, h)),
        out_shape=jax.ShapeDtypeStruct(qg.shape, BF16),
        compiler_params=_params(3),
        name="lat_gqa",
    )(qg, kg, vg, cache_k, cache_v)


def _lat_diff_kernel(lam_ref, q_ref, k_ref, v_ref, ck_ref, cv_ref, subg_ref, o_ref):
    lam = lam_ref[0]
    c1 = slice(0, HEAD_DIM)
    c2 = slice(HEAD_DIM, 2 * HEAD_DIM)
    ck = ck_ref[...].astype(BF16)
    (e1_l, e1_c), inv1 = _softmax_parts([_nt_dot(q_ref[:, c1], k_ref[:, c1]), _nt_dot(q_ref[:, c1], ck[:, c1])])
    (e2_l, e2_c), inv2 = _softmax_parts([_nt_dot(q_ref[:, c2], k_ref[:, c2]), _nt_dot(q_ref[:, c2], ck[:, c2])])
    w2 = lam * inv2
    a_l = e1_l * inv1 - w2 * e2_l
    a_c = e1_c * inv1 - w2 * e2_c
    o = jnp.dot(a_l.astype(BF16), v_ref[...], preferred_element_type=F32)
    o += jnp.dot(a_c.astype(BF16), cv_ref[...].astype(BF16), preferred_element_type=F32)
    o_ref[...] = _subln(o, subg_ref[...]).astype(o_ref.dtype)


def _lat_diff_attention(lam, qd, kd, vd, cache_k, cache_v, subg, n_batch, seq):
    tq = ATTN_TQ
    nq = seq // tq
    past = cache_k.shape[1]
    hw = 2 * HEAD_DIM
    return pl.pallas_call(
        _lat_diff_kernel,
        grid=(n_batch, DIFF_HEADS, nq),
        in_specs=[pl.BlockSpec(memory_space=pltpu.SMEM),
                  pl.BlockSpec((tq, hw), lambda b, h, q: (b * nq + q, h)),
                  pl.BlockSpec((seq, hw), lambda b, h, q: (b, h)),
                  pl.BlockSpec((seq, DIFF_V_DIM), lambda b, h, q: (b, h)),
                  pl.BlockSpec((None, past, hw), lambda b, h, q: (b, 0, h)),
                  pl.BlockSpec((None, past, DIFF_V_DIM), lambda b, h, q: (b, 0, h)),
                  pl.BlockSpec((1, DIFF_V_DIM), lambda b, h, q: (0, 0))],
        out_specs=pl.BlockSpec((tq, DIFF_V_DIM), lambda b, h, q: (b * nq + q, h)),
        out_shape=jax.ShapeDtypeStruct(qd.shape, BF16),
        compiler_params=_params(3),
        name="lat_diff",
    )(lam, qd, kd, vd, cache_k, cache_v, subg)


def _merge_kernel(x_ref, og_ref, od_ref, gate_ref, wa_ref, wb_ref, wo_ref, mod_ref, g2_ref, x1_ref, h2t_ref):
    ya = jnp.dot(og_ref[...], wa_ref[...], preferred_element_type=F32)
    yb = jnp.dot(od_ref[...], wb_ref[...], preferred_element_type=F32)
    merged = gate_ref[:, :D_MODEL].astype(F32) * ya + gate_ref[:, D_MODEL:].astype(F32) * yb
    y = jnp.dot(merged.astype(BF16), wo_ref[...], preferred_element_type=F32)
    x1 = x_ref[...] + mod_ref[0, 2:3, :] * y
    x1_ref[...] = x1
    h2 = (_rms(x1) * g2_ref[...]) * (1.0 + mod_ref[0, 4:5, :]) + mod_ref[0, 3:4, :]
    h2t_ref[...] = h2.T.astype(BF16)


def _merge(x, og, od, gates, wa, wb, wo, mod, mod_row_fn, g2):
    n, d = x.shape
    tm = MERGE_TM
    resident = lambda shape: pl.BlockSpec(shape, lambda i: (0, 0), pipeline_mode=pl.Buffered(1))
    return pl.pallas_call(
        _merge_kernel,
        grid=(n // tm,),
        in_specs=[pl.BlockSpec((tm, d), lambda i: (i, 0)),
                  pl.BlockSpec((tm, GQA_Q_W), lambda i: (i, 0)),
                  pl.BlockSpec((tm, DIFF_V_W), lambda i: (i, 0)),
                  pl.BlockSpec((tm, GATE_W), lambda i: (i, 0)),
                  resident(wa.shape), resident(wb.shape), resident(wo.shape),
                  pl.BlockSpec((1, 6, d), lambda i: (mod_row_fn(i), 0, 0)),
                  pl.BlockSpec((1, d), lambda i: (0, 0))],
        out_specs=[pl.BlockSpec((tm, d), lambda i: (i, 0)),
                   pl.BlockSpec((d, tm), lambda i: (0, i))],
        out_shape=[jax.ShapeDtypeStruct((n, d), F32), jax.ShapeDtypeStruct((d, n), BF16)],
        compiler_params=_params(1),
        name="merge",
    )(x, og, od, gates, wa, wb, wo, mod, g2)


def _top_values(s, k):
    rows = s.shape[0]
    iota = lax.broadcasted_iota(jnp.int32, s.shape, 0).astype(F32)
    vals = []
    for r in range(k):
        m = jnp.max(s, axis=0, keepdims=True)
        vals.append(m)
        if r + 1 < k:
            first = jnp.min(jnp.where(s == m, iota, float(rows)), axis=0, keepdims=True)
            s = jnp.where(iota == first, -jnp.inf, s)
    return jnp.concatenate(vals, axis=0)


def _peer_score_kernel(h2t_ref, wqt_ref, keys_ref, s_ref, sel_ref):
    qt = jnp.dot(wqt_ref[...], h2t_ref[...], preferred_element_type=F32).astype(BF16)
    taus = []
    shifts = []
    for h in range(PEER_HEADS):
        r1 = slice((2 * h) * PEER_N_KEYS, (2 * h + 1) * PEER_N_KEYS)
        r2 = slice((2 * h + 1) * PEER_N_KEYS, (2 * h + 2) * PEER_N_KEYS)
        s1 = jnp.dot(keys_ref[2 * h], qt[r1], preferred_element_type=F32)
        s2 = jnp.dot(keys_ref[2 * h + 1], qt[r2], preferred_element_type=F32)
        top1 = _top_values(s1, PEER_TOPK)
        top2 = _top_values(s2, PEER_TOPK)
        cand = jnp.concatenate([top1[r:r + 1] + top2 for r in range(PEER_TOPK)], axis=0)
        tops = _top_values(cand, PEER_TOPK)
        z = jnp.sum(jnp.exp(tops - tops[0:1]), axis=0, keepdims=True)
        s_ref[r1, :] = s1
        s_ref[r2, :] = s2
        taus.append(tops[PEER_TOPK - 1:PEER_TOPK])
        shifts.append(tops[0:1] + jnp.log(z))
    sel_ref[...] = jnp.concatenate(taus + shifts, axis=0)


def _peer_scores(h2t, wqt, keys):
    d, n = h2t.shape
    tb = SCORE_TB
    rows = PEER_HEADS * 2 * PEER_N_KEYS
    return pl.pallas_call(
        _peer_score_kernel,
        grid=(n // tb,),
        in_specs=[pl.BlockSpec((d, tb), lambda i: (0, i)),
                  pl.BlockSpec(wqt.shape, lambda i: (0, 0), pipeline_mode=pl.Buffered(1)),
                  pl.BlockSpec(keys.shape, lambda i: (0, 0, 0), pipeline_mode=pl.Buffered(1))],
        out_specs=[pl.BlockSpec((rows, tb), lambda i: (0, i)),
                   pl.BlockSpec((2 * PEER_HEADS, tb), lambda i: (0, i))],
        out_shape=[jax.ShapeDtypeStruct((rows, n), F32),
                   jax.ShapeDtypeStruct((2 * PEER_HEADS, n), F32)],
        compiler_params=_params(1),
        name="peer_score",
    )(h2t, wqt, keys)


def _expert_coefs(at_ref, coef_ref, s_ref, sel_ref, key_group, first_row):
    eb, nb = at_ref.shape
    for al in range(eb // PEER_N_KEYS):
        row = first_row + al
        for t in range(nb // LANES):
            ls = slice(t * LANES, (t + 1) * LANES)
            pick = lax.broadcasted_iota(jnp.int32, (SUBLANES, LANES), 0) == row
            s1b = [jnp.sum(jnp.where(pick, s_ref[pl.ds((2 * h) * PEER_N_KEYS + key_group, SUBLANES), ls], 0.0),
                           axis=0, keepdims=True) for h in range(PEER_HEADS)]
            taub = [sel_ref[h:h + 1, ls] for h in range(PEER_HEADS)]
            s1c = [s1b[h] - sel_ref[PEER_HEADS + h:PEER_HEADS + h + 1, ls] for h in range(PEER_HEADS)]
            for c in range(PEER_N_KEYS // DENSE_RB):
                w = jnp.zeros((DENSE_RB, LANES), F32)
                for h in range(PEER_HEADS):
                    rows = slice((2 * h + 1) * PEER_N_KEYS + c * DENSE_RB, (2 * h + 1) * PEER_N_KEYS + (c + 1) * DENSE_RB)
                    s2 = s_ref[rows, ls]
                    w = w + jnp.where(s1b[h] + s2 >= taub[h], jnp.exp(s1c[h] + s2), 0.0)
                out_rows = slice(al * PEER_N_KEYS + c * DENSE_RB, al * PEER_N_KEYS + (c + 1) * DENSE_RB)
                pre = at_ref[out_rows, ls]
                act = 0.5 * pre * (1.0 + lax.erf(pre * math.sqrt(0.5)))
                coef_ref[out_rows, ls] = (act * w).astype(BF16)


def _dense_work_item(t, lag, n_items, n_blocks):
    item = jnp.clip(t - lag, 0, n_items - 1)
    return item // n_blocks, item % n_blocks


def _peer_dense_kernel(h2t_ref, u_ref, vt_ref, s_ref, sel_ref, acc_ref, at0_ref, at1_ref, coef0_ref, coef1_ref,
                       *, n_blocks):
    t = pl.program_id(0)
    n_items = pl.num_programs(0) - 2
    _, weights_block = _dense_work_item(t, 1, n_items, n_blocks)
    _, acc_block = _dense_work_item(t, 2, n_items, n_blocks)
    key_group = pl.multiple_of(weights_block * SUBLANES, SUBLANES)
    assert at0_ref.shape[0] == SUBLANES * PEER_N_KEYS

    @pl.when(t == 0)
    def _():
        at1_ref[...] = jnp.zeros_like(at1_ref)
        coef0_ref[...] = jnp.zeros_like(coef0_ref)

    @pl.when(acc_block == 0)
    def _():
        acc_ref[...] = jnp.zeros_like(acc_ref)

    def stages(at_new, at_prev, coef_new, coef_prev):
        at_new[...] = jnp.dot(u_ref[...], h2t_ref[...], preferred_element_type=F32)
        acc_ref[...] += jnp.dot(vt_ref[...], coef_prev[...], preferred_element_type=F32)
        _expert_coefs(at_prev, coef_new, s_ref, sel_ref, key_group, 0)

    @pl.when(t % 2 == 0)
    def _():
        stages(at0_ref, at1_ref, coef1_ref, coef0_ref)

    @pl.when(t % 2 == 1)
    def _():
        stages(at1_ref, at0_ref, coef0_ref, coef1_ref)


def _peer_dense(h2t, u, vt, s, sel):
    d, n = h2t.shape
    nb, eb = DENSE_NB, DENSE_EB
    rows = s.shape[0]
    n_blocks = PEER_N_EXPERTS // eb
    n_items = (n // nb) * n_blocks
    item = functools.partial(_dense_work_item, n_items=n_items, n_blocks=n_blocks)
    return pl.pallas_call(
        functools.partial(_peer_dense_kernel, n_blocks=n_blocks),
        grid=(n_items + 2,),
        in_specs=[pl.BlockSpec((d, nb), lambda t: (0, item(t, 0)[0])),
                  pl.BlockSpec((eb, d), lambda t: (item(t, 0)[1], 0)),
                  pl.BlockSpec((d, eb), lambda t: (0, item(t, 2)[1])),
                  pl.BlockSpec((rows, nb), lambda t: (0, item(t, 1)[0])),
                  pl.BlockSpec((2 * PEER_HEADS, nb), lambda t: (0, item(t, 1)[0]))],
        out_specs=pl.BlockSpec((d, nb), lambda t: (0, item(t, 2)[0])),
        out_shape=jax.ShapeDtypeStruct((d, n), F32),
        scratch_shapes=[pltpu.VMEM((eb, nb), F32), pltpu.VMEM((eb, nb), F32),
                        pltpu.VMEM((eb, nb), BF16), pltpu.VMEM((eb, nb), BF16)],
        compiler_params=_params(1),
        name="peer_dense",
    )(h2t, u, vt, s, sel)


def _final_kernel(x1_ref, pt_ref, mod_ref, g_ref, o_ref):
    x2 = x1_ref[...] + mod_ref[0, 5:6, :] * pt_ref[...].T
    o_ref[...] = _rms(x2) * g_ref[...]


def _final(x1, peer_t, mod, mod_row_fn, g):
    n, d = x1.shape
    tm = FINAL_TM
    return pl.pallas_call(
        _final_kernel,
        grid=(n // tm,),
        in_specs=[pl.BlockSpec((tm, d), lambda i: (i, 0)),
                  pl.BlockSpec((d, tm), lambda i: (0, i)),
                  pl.BlockSpec((1, 6, d), lambda i: (mod_row_fn(i), 0, 0)),
                  pl.BlockSpec((1, d), lambda i: (0, 0))],
        out_specs=pl.BlockSpec((tm, d), lambda i: (i, 0)),
        out_shape=jax.ShapeDtypeStruct((n, d), F32),
        compiler_params=_params(1),
        name="final",
    )(x1, peer_t, mod, g)


def _rope_tables(n_tokens):
    rows = n_tokens // GRID_W
    row = jnp.repeat(jnp.arange(rows, dtype=F32), GRID_W)
    col = jnp.tile(jnp.arange(GRID_W, dtype=F32), rows)
    inv = ROPE_THETA ** (-jnp.arange(ROPE_PAIRS_PER_AXIS, dtype=F32) / ROPE_PAIRS_PER_AXIS)
    ar = row[:, None] * inv
    ac = col[:, None] * inv
    ang = jnp.concatenate([ar, ar, ac, ac], axis=-1)
    sign = jnp.where((jnp.arange(HEAD_DIM) // (HEAD_DIM // 4)) % 2 == 0, -1.0, 1.0).astype(F32)
    return jnp.cos(ang), jnp.sin(ang) * sign


def _trunk(x, mod, rows_per_mod, first_mod_row, w, rope_tables, attention):
    def mod_row_for(tm):
        return lambda i: first_mod_row + (i * tm) // rows_per_mod

    kv_dtype = BF16 if rope_tables is not None else F32
    qg, kg, vg, qd, kd, vd, gates = _inproj(x, mod, mod_row_for(INPROJ_TM), w["norm1_g"], w["w_in"],
                                            w["q_norm_g"], w["k_norm_g"], rope_tables, kv_dtype)
    og, od = attention(qg, kg, vg, qd, kd, vd)
    x1, h2t = _merge(x, og, od, gates, w["w_a"], w["w_b"], w["w_o"], mod, mod_row_for(MERGE_TM), w["norm2_g"])
    s, sel = _peer_scores(h2t, w["wq_t"], w["keys"])
    peer_t = _peer_dense(h2t, w["u"], w["v_t"], s, sel)
    y = _final(x1, peer_t, mod, mod_row_for(FINAL_TM), w["final_g"])
    return y, (kg, vg, kd, vd)


def kernel(x_prompt, x_sample, c, cache_gqa_k, cache_gqa_v, cache_diff_k, cache_diff_v, c_ctx, w_ada, b_ada, norm1_g, norm2_g, w_in, q_norm_g, k_norm_g, lambda_q1, lambda_k1, lambda_q2, lambda_k2, diff_subln_g, w_branch_a, w_branch_b, w_out, peer_w_query, peer_sub_keys, peer_u, peer_v, final_norm_g):
    assert w_in.shape[0] == 1, "one trunk layer"
    batch, seq, d = x_prompt.shape
    dec_batch, dec_seq, _ = x_sample.shape
    past = cache_gqa_k.shape[2]
    l = 0

    w = {
        "norm1_g": norm1_g[l][None, :], "norm2_g": norm2_g[l][None, :], "final_g": final_norm_g[None, :],
        "q_norm_g": q_norm_g[l][None, :], "k_norm_g": k_norm_g[l][None, :],
        "w_in": w_in[l].astype(BF16),
        "w_a": w_branch_a[l].astype(BF16), "w_b": w_branch_b[l].astype(BF16), "w_o": w_out[l].astype(BF16),
        "wq_t": peer_w_query[l].T.astype(BF16),
        "keys": peer_sub_keys[l].reshape(PEER_HEADS * 2, PEER_N_KEYS, PEER_HALF).astype(BF16),
        "u": peer_u[l].astype(BF16),
        "v_t": peer_v[l].T.astype(BF16),
    }
    subg = diff_subln_g[l][None, :]
    lam = (jnp.exp(jnp.sum((lambda_q1[l] * lambda_k1[l]).astype(F32)))
           - jnp.exp(jnp.sum((lambda_q2[l] * lambda_k2[l]).astype(F32))) + LAMBDA_INIT).reshape(1)

    cond = jnp.concatenate([c_ctx[None, :], c], axis=0)
    n_cond = cond.shape[0]
    cond = jnp.pad(cond, ((0, SUBLANES - n_cond), (0, 0)))
    mod = _ada(cond, w_ada[l], b_ada[l][None, :])[:n_cond].reshape(n_cond, 6, d)

    def ctx_attention(qg, kg, vg, qd, kd, vd):
        return _ctx_attention(lam, qg, kg, vg, qd, kd, vd, subg, seq)

    ck_g = cache_gqa_k[:, l].reshape(dec_batch, past, GQA_KV_W)
    cv_g = cache_gqa_v[:, l].reshape(dec_batch, past, GQA_KV_W)
    ck_d = cache_diff_k[:, l].reshape(dec_batch, past, DIFF_QK_W)
    cv_d = cache_diff_v[:, l].reshape(dec_batch, past, DIFF_V_W)

    def lat_attention(qg, kg, vg, qd, kd, vd):
        og = _lat_gqa_attention(qg, kg, vg, ck_g, cv_g, dec_batch, dec_seq)
        od = _lat_diff_attention(lam, qd, kd, vd, ck_d, cv_d, subg, dec_batch, dec_seq)
        return og, od

    n_ctx = batch * seq
    yp, (kg, vg, kd, vd) = _trunk(x_prompt.reshape(n_ctx, d), mod, n_ctx, 0, w, None, ctx_attention)
    ys, _ = _trunk(x_sample.reshape(dec_batch * dec_seq, d), mod, dec_seq, 1, w,
                   _rope_tables(dec_seq), lat_attention)

    return (yp.reshape(batch, seq, d),
            ys.reshape(dec_batch, dec_seq, d),
            kg.reshape(batch, 1, seq, GQA_KV_HEADS, HEAD_DIM),
            vg.reshape(batch, 1, seq, GQA_KV_HEADS, HEAD_DIM),
            kd.reshape(batch, 1, seq, DIFF_HEADS, 2, HEAD_DIM),
            vd.reshape(batch, 1, seq, DIFF_HEADS, DIFF_V_DIM))
```
